```python
import jax, jax.numpy as jnp
from jax import lax
import numpy as np

D_MODEL = 1024
BATCH = 8
SEQ = 2048
DEPTH = 1

CHUNK = 64
D_MIX = D_MODEL
D_HGRN = D_MIX // 2
D_CONV = D_MIX - D_HGRN
HGRN_HEAD_DIM = 128
HGRN_HEADS = D_HGRN // HGRN_HEAD_DIM
CONV_WIDTH = 3
CONV_GROUPS = 8
EPS = 1e-6
IN_COLS = 4 * D_HGRN + 4 * D_CONV

kernel_name = "hgrn2_shortconv_parallel_hybrid"


def rmsnorm(x, g):
    xf = x.astype(jnp.float32)
    y = xf * lax.rsqrt(jnp.mean(xf * xf, axis=-1, keepdims=True) + EPS)
    return (y * g.astype(jnp.float32)).astype(x.dtype)


def grouped_rmsnorm(x, g, n_groups):
    bsz, s, w = x.shape
    xf = x.astype(jnp.float32).reshape(bsz, s, n_groups, w // n_groups)
    y = xf * lax.rsqrt(jnp.mean(xf * xf, axis=-1, keepdims=True) + EPS)
    return (y.reshape(bsz, s, w) * g.astype(jnp.float32)).astype(x.dtype)


def hgrn2_chunkwise(q, log_f, k, v):
    bsz, s, h, dk = q.shape
    dv = v.shape[-1]
    nc = s // CHUNK

    def split(t):
        return t.reshape(bsz, nc, CHUNK, h, t.shape[-1]).transpose(0, 3, 1, 2, 4)

    q, log_f, k, v = (split(t).astype(jnp.float32) for t in (q, log_f, k, v))
    b = jnp.cumsum(log_f, axis=3)
    g = b[..., -1:, :]
    q_dec = q * jnp.exp(b)
    k_inv = k * jnp.exp(-b)
    k_end = k * jnp.exp(g - b)
    causal = jnp.tril(jnp.ones((CHUNK, CHUNK), dtype=bool))
    scores = jnp.einsum('bhnck,bhnsk->bhncs', q_dec, k_inv)
    scores = jnp.where(causal, scores, 0.0)
    o_intra = jnp.einsum('bhncs,bhnsv->bhncv', scores, v)
    chunk_update = jnp.einsum('bhnsk,bhnsv->bhnkv', k_end, v)
    chunk_decay = jnp.exp(g[..., 0, :])

    def step(state, inp):
        dec, upd = inp
        return dec[..., None] * state + upd, state

    s0 = jnp.zeros((bsz, h, dk, dv), jnp.float32)
    _, s_before = lax.scan(step, s0, (jnp.moveaxis(chunk_decay, 2, 0),
                                      jnp.moveaxis(chunk_update, 2, 0)))
    s_before = jnp.moveaxis(s_before, 0, 2)
    o_inter = jnp.einsum('bhnck,bhnkv->bhncv', q_dec, s_before)
    o = o_intra + o_inter
    return o.transpose(0, 2, 3, 1, 4).reshape(bsz, s, h, dv)


def causal_depthwise_conv(u, w):
    c = u.shape[-1]
    return lax.conv_general_dilated(
        u, w.astype(u.dtype)[:, None, :], window_strides=(1,),
        padding=[(CONV_WIDTH - 1, 0)],
        dimension_numbers=('NWC', 'WIO', 'NWC'), feature_group_count=c)


def setup_inputs(seed: int = 0) -> dict:
    key = jax.random.key(seed)
    ks = jax.random.split(key, 9)
    f32 = jnp.float32
    x = jax.random.normal(ks[0], (BATCH, SEQ, D_MODEL), f32)
    norm_gain = 1.0 + 0.02 * jax.random.normal(ks[1], (DEPTH, D_MODEL), f32)
    w_in = jax.random.normal(ks[2], (DEPTH, D_MODEL, IN_COLS), f32) * D_MODEL ** -0.5
    lb_logits = 1.0 + 0.1 * jax.random.normal(ks[3], (DEPTH + 1, D_HGRN), f32)
    conv_w = jax.random.normal(ks[4], (DEPTH, CONV_WIDTH, D_CONV), f32) * CONV_WIDTH ** -0.5
    hgrn_norm_gain = 1.0 + 0.02 * jax.random.normal(ks[5], (DEPTH, D_HGRN), f32)
    conv_norm_gain = 1.0 + 0.02 * jax.random.normal(ks[6], (DEPTH, D_CONV), f32)
    w_out = jax.random.normal(ks[7], (DEPTH, D_MIX, D_MODEL), f32) * D_MIX ** -0.5
    final_norm_gain = 1.0 + 0.02 * jax.random.normal(ks[8], (D_MODEL,), f32)
    return {"x": x, "norm_gain": norm_gain, "w_in": w_in, "lb_logits": lb_logits,
            "conv_w": conv_w, "hgrn_norm_gain": hgrn_norm_gain,
            "conv_norm_gain": conv_norm_gain, "w_out": w_out,
            "final_norm_gain": final_norm_gain}


def reference(x, norm_gain, w_in, lb_logits, conv_w, hgrn_norm_gain,
              conv_norm_gain, w_out, final_norm_gain):
    bsz, s, _ = x.shape
    lower_bounds = jnp.cumsum(jax.nn.softmax(lb_logits.astype(jnp.float32), axis=0), axis=0)
    splits = [D_HGRN, 2 * D_HGRN, 3 * D_HGRN, 4 * D_HGRN,
              4 * D_HGRN + D_CONV, 4 * D_HGRN + 2 * D_CONV, 4 * D_HGRN + 3 * D_CONV]
    for l in range(DEPTH):
        h = rmsnorm(x, norm_gain[l])
        proj = jnp.einsum('bsd,dc->bsc', h, w_in[l])
        q, f_logit, i_in, z_a, u, gate_b, gate_c, z_b = jnp.split(proj, splits, axis=-1)

        lb = lower_bounds[l]
        f = lb + (1.0 - lb) * jax.nn.sigmoid(f_logit.astype(jnp.float32))
        log_f = jnp.log(f)
        k = 1.0 - f
        hs = (bsz, s, HGRN_HEADS, HGRN_HEAD_DIM)
        o_a = hgrn2_chunkwise(q.reshape(hs), log_f.reshape(hs), k.reshape(hs), i_in.reshape(hs))
        o_a = grouped_rmsnorm(o_a.reshape(bsz, s, D_HGRN).astype(x.dtype),
                              hgrn_norm_gain[l], HGRN_HEADS)
        o_a = o_a * jax.nn.silu(z_a)

        y_b = gate_b * causal_depthwise_conv(gate_c * u, conv_w[l])
        o_b = grouped_rmsnorm(y_b, conv_norm_gain[l], CONV_GROUPS) * jax.nn.silu(z_b)

        mixed = jnp.concatenate([o_a, o_b], axis=-1)
        x = x + jnp.einsum('bsc,cd->bsd', mixed, w_out[l])
    return rmsnorm(x, final_norm_gain)
```

```python
import functools

import jax
import jax.numpy as jnp
from jax import lax
from jax.experimental import pallas as pl
from jax.experimental.pallas import tpu as pltpu

D_HGRN = 512
D_CONV = 512
HEAD_DIM = 128
N_HEADS = D_HGRN // HEAD_DIM
CHUNK = 64
CONV_WIDTH = 3
CONV_GROUP_DIM = 64
EPS = 1e-6
SEQ_TILE = 512
LANES = 128
SUBLANES = 8
VMEM_LIMIT_BYTES = 60000 * 1024

_NT = (((1,), (1,)), ((), ()))
_TN = (((0,), (0,)), ((), ()))


def _silu(z):
    return z * jax.nn.sigmoid(z)


def _block_kernel(x_ref, g_ref, win_ref, lbl_ref, cw_ref, hg_ref, cg_ref, wout_ref, fg_ref,
                  out_ref, st_ref, cu_ref, q_s, lf_s, k_s, v_s, o_s):
    f32, bf16 = jnp.float32, jnp.bfloat16
    tile = x_ref.shape[0]

    @pl.when(pl.program_id(1) == 0)
    def _reset_carries():
        st_ref[...] = jnp.zeros_like(st_ref)
        cu_ref[0:SUBLANES, :] = jnp.zeros((SUBLANES, D_CONV), f32)

    x = x_ref[...]
    ms = jnp.mean(x * x, axis=-1, keepdims=True)
    h = (x * lax.rsqrt(ms + EPS) * g_ref[...]).astype(bf16)

    def proj(j, width):
        return jnp.dot(h, win_ref[:, j * width:(j + 1) * width], preferred_element_type=f32)

    lbl = lbl_ref[...]
    lmax = jnp.max(lbl, axis=0, keepdims=True)
    lexp = jnp.exp(lbl - lmax)
    lb = lexp[0:1, :] / jnp.sum(lexp, axis=0, keepdims=True)

    q_s[...] = proj(0, D_HGRN)
    f = lb + (1.0 - lb) * jax.nn.sigmoid(proj(1, D_HGRN))
    lf_s[...] = jnp.log(f)
    k_s[...] = 1.0 - f
    v_s[...] = proj(2, D_HGRN).astype(bf16)

    row = lax.broadcasted_iota(jnp.int32, (CHUNK, CHUNK), 0)
    col = lax.broadcasted_iota(jnp.int32, (CHUNK, CHUNK), 1)
    causal = row >= col
    tri = causal.astype(bf16)

    def chunk_step(c, carry):
        r = pl.ds(pl.multiple_of(c * CHUNK, CHUNK), CHUNK)
        lf = lf_s[r, :]
        lf_hi = lf.astype(bf16)
        lf_lo = (lf - lf_hi.astype(f32)).astype(bf16)
        b = (jnp.dot(tri, lf_hi, preferred_element_type=f32)
             + jnp.dot(tri, lf_lo, preferred_element_type=f32))
        eb = jnp.exp(b)
        k_inv = k_s[r, :] * jnp.exp(-b)
        decay = eb[CHUNK - 1:CHUNK, :]
        q_dec = (q_s[r, :] * eb).astype(bf16)
        k_end = (k_inv * decay).astype(bf16)
        k_inv = k_inv.astype(bf16)
        v = v_s[r, :]
        for hd in range(N_HEADS):
            sl = slice(hd * HEAD_DIM, (hd + 1) * HEAD_DIM)
            scores = lax.dot_general(q_dec[:, sl], k_inv[:, sl], _NT, preferred_element_type=f32)
            scores = jnp.where(causal, scores, 0.0).astype(bf16)
            state_t = st_ref[hd]
            o = (jnp.dot(scores, v[:, sl], preferred_element_type=f32)
                 + lax.dot_general(q_dec[:, sl], state_t.astype(bf16), _NT,
                                   preferred_element_type=f32))
            o_s[r, sl] = o
            upd_t = lax.dot_general(v[:, sl], k_end[:, sl], _TN, preferred_element_type=f32)
            st_ref[hd] = state_t * decay[:, sl] + upd_t
        return carry

    lax.fori_loop(0, tile // CHUNK, chunk_step, 0)

    hg = hg_ref[...]
    o_parts = []
    for hd in range(N_HEADS):
        sl = slice(hd * HEAD_DIM, (hd + 1) * HEAD_DIM)
        oh = o_s[:, sl]
        oms = jnp.mean(oh * oh, axis=-1, keepdims=True)
        o_parts.append(oh * lax.rsqrt(oms + EPS) * hg[:, sl])
    o_a = (jnp.concatenate(o_parts, axis=-1) * _silu(proj(3, D_HGRN))).astype(bf16)

    cw = cw_ref[...]
    cu = proj(6, D_CONV) * proj(4, D_CONV)
    cu_ref[SUBLANES:tile + SUBLANES, :] = cu
    conv = (cw[2:3, :] * cu
            + cw[1:2, :] * cu_ref[SUBLANES - 1:tile + SUBLANES - 1, :]
            + cw[0:1, :] * cu_ref[SUBLANES - 2:tile + SUBLANES - 2, :])
    cu_ref[0:SUBLANES, :] = cu_ref[tile:tile + SUBLANES, :]
    y = proj(5, D_CONV) * conv
    cg = cg_ref[...]
    lane = lax.broadcasted_iota(jnp.int32, (1, LANES), 1)
    first_group = lane < CONV_GROUP_DIM
    y_parts = []
    for j in range(D_CONV // LANES):
        sl = slice(j * LANES, (j + 1) * LANES)
        yj = y[:, sl]
        y2 = yj * yj
        tot = jnp.sum(y2, axis=-1, keepdims=True)
        lo = jnp.sum(jnp.where(first_group, y2, 0.0), axis=-1, keepdims=True)
        yms = jnp.where(first_group, lo, tot - lo) * (1.0 / CONV_GROUP_DIM)
        y_parts.append(yj * lax.rsqrt(yms + EPS) * cg[:, sl])
    o_b = (jnp.concatenate(y_parts, axis=-1) * _silu(proj(7, D_CONV))).astype(bf16)

    mix = (jnp.dot(o_a, wout_ref[0:D_HGRN, :], preferred_element_type=f32)
           + jnp.dot(o_b, wout_ref[D_HGRN:D_HGRN + D_CONV, :], preferred_element_type=f32))
    res = x + mix
    rms = jnp.mean(res * res, axis=-1, keepdims=True)
    out_ref[...] = res * lax.rsqrt(rms + EPS) * fg_ref[...]


@jax.jit
def kernel(x, norm_gain, w_in, lb_logits, conv_w, hgrn_norm_gain, conv_norm_gain, w_out,
           final_norm_gain):
    bsz, seq, d_model = x.shape
    depth = norm_gain.shape[0]
    assert depth == 1 and lb_logits.shape == (depth + 1, D_HGRN)
    assert w_in.shape == (depth, d_model, 4 * D_HGRN + 4 * D_CONV)
    assert w_out.shape == (depth, D_HGRN + D_CONV, d_model)
    tile = min(SEQ_TILE, seq)
    assert seq % tile == 0 and tile % CHUNK == 0

    full = lambda shape: pl.BlockSpec(shape, lambda b, s: (0,) * len(shape))
    in_cols = w_in.shape[2]
    grid_spec = pltpu.PrefetchScalarGridSpec(
        num_scalar_prefetch=0,
        grid=(bsz, seq // tile),
        in_specs=[
            pl.BlockSpec((None, tile, d_model), lambda b, s: (b, s, 0)),
            full((1, d_model)),
            full((d_model, in_cols)),
            full((depth + 1, D_HGRN)),
            full((CONV_WIDTH, D_CONV)),
            full((1, D_HGRN)),
            full((1, D_CONV)),
            full((D_HGRN + D_CONV, d_model)),
            full((1, d_model)),
        ],
        out_specs=pl.BlockSpec((None, tile, d_model), lambda b, s: (b, s, 0)),
        scratch_shapes=[
            pltpu.VMEM((N_HEADS, HEAD_DIM, HEAD_DIM), jnp.float32),
            pltpu.VMEM((tile + SUBLANES, D_CONV), jnp.float32),
            pltpu.VMEM((tile, D_HGRN), jnp.float32),
            pltpu.VMEM((tile, D_HGRN), jnp.float32),
            pltpu.VMEM((tile, D_HGRN), jnp.float32),
            pltpu.VMEM((tile, D_HGRN), jnp.bfloat16),
            pltpu.VMEM((tile, D_HGRN), jnp.float32),
        ],
    )
    return pl.pallas_call(
        _block_kernel,
        grid_spec=grid_spec,
        out_shape=jax.ShapeDtypeStruct(x.shape, x.dtype),
        compiler_params=pltpu.CompilerParams(
            dimension_semantics=("arbitrary", "arbitrary"),
            vmem_limit_bytes=VMEM_LIMIT_BYTES),
        name="hgrn2_shortconv_block",
    )(x, norm_gain, w_in[0].astype(jnp.bfloat16), lb_logits, conv_w[0], hgrn_norm_gain,
      conv_norm_gain, w_out[0].astype(jnp.bfloat16), final_norm_gain.reshape(1, d_model))
```

```python
import functools

import jax
import jax.numpy as jnp
from jax import lax
from jax.experimental import pallas as pl
from jax.experimental.pallas import tpu as pltpu

D_HGRN = 512
D_CONV = 512
HEAD_DIM = 128
N_HEADS = D_HGRN // HEAD_DIM
CHUNK = 64
CONV_WIDTH = 3
CONV_GROUP_DIM = 64
EPS = 1e-6
SEQ_TILE = 512
LANES = 128
SUBLANES = 8
VMEM_LIMIT_BYTES = 60000 * 1024

_NT = (((1,), (1,)), ((), ()))
_TN = (((0,), (0,)), ((), ()))


def _silu(z):
    return z * jax.nn.sigmoid(z)


def _block_kernel(x_ref, g_ref, win_ref, lbl_ref, cw_ref, hg_ref, cg_ref, wout_ref, fg_ref,
                  out_ref, st_ref, cu_ref, o_s):
    f32, bf16 = jnp.float32, jnp.bfloat16
    tile = x_ref.shape[0]

    @pl.when(pl.program_id(1) == 0)
    def _reset_carries():
        st_ref[...] = jnp.zeros_like(st_ref)
        cu_ref[0:SUBLANES, :] = jnp.zeros((SUBLANES, D_CONV), f32)

    x = x_ref[...]
    ms = jnp.mean(x * x, axis=-1, keepdims=True)
    h = (x * lax.rsqrt(ms + EPS) * g_ref[...]).astype(bf16)

    def proj(j, width):
        return jnp.dot(h, win_ref[:, j * width:(j + 1) * width], preferred_element_type=f32)

    lbl = lbl_ref[...]
    lmax = jnp.max(lbl, axis=0, keepdims=True)
    lexp = jnp.exp(lbl - lmax)
    lb = lexp[0:1, :] / jnp.sum(lexp, axis=0, keepdims=True)

    q = proj(0, D_HGRN)
    f = lb + (1.0 - lb) * jax.nn.sigmoid(proj(1, D_HGRN))
    lf = jnp.log(f)
    k = 1.0 - f
    v = proj(2, D_HGRN).astype(bf16)

    row = lax.broadcasted_iota(jnp.int32, (CHUNK, CHUNK), 0)
    col = lax.broadcasted_iota(jnp.int32, (CHUNK, CHUNK), 1)
    causal = row >= col
    tri = causal.astype(bf16)
    n_chunks = tile // CHUNK
    rows = [slice(c * CHUNK, (c + 1) * CHUNK) for c in range(n_chunks)]
    heads = [slice(hd * HEAD_DIM, (hd + 1) * HEAD_DIM) for hd in range(N_HEADS)]

    lf_hi = lf.astype(bf16)
    lf_lo = (lf - lf_hi.astype(f32)).astype(bf16)
    b = jnp.concatenate(
        [jnp.dot(tri, lf_hi[r], preferred_element_type=f32)
         + jnp.dot(tri, lf_lo[r], preferred_element_type=f32) for r in rows], axis=0)
    eb = jnp.exp(b)
    q_dec = (q * eb).astype(bf16)
    k_inv = k * jnp.exp(-b)
    decay = [eb[r][CHUNK - 1:CHUNK, :] for r in rows]
    k_end = [(k_inv[r] * decay[c]).astype(bf16) for c, r in enumerate(rows)]
    k_inv = k_inv.astype(bf16)

    scores = [[lax.dot_general(q_dec[r, sl], k_inv[r, sl], _NT, preferred_element_type=f32)
               for sl in heads] for r in rows]
    upd_t = [[lax.dot_general(v[r, sl], k_end[c][:, sl], _TN, preferred_element_type=f32)
              for sl in heads] for c, r in enumerate(rows)]
    state_before = [[None] * N_HEADS for _ in rows]
    for hd, sl in enumerate(heads):
        state_t = st_ref[hd]
        for c in range(n_chunks):
            state_before[c][hd] = state_t.astype(bf16)
            state_t = state_t * decay[c][:, sl] + upd_t[c][hd]
        st_ref[hd] = state_t
    for c, r in enumerate(rows):
        for hd, sl in enumerate(heads):
            sc = jnp.where(causal, scores[c][hd], 0.0).astype(bf16)
            o_s[r, sl] = (jnp.dot(sc, v[r, sl], preferred_element_type=f32)
                          + lax.dot_general(q_dec[r, sl], state_before[c][hd], _NT,
                                            preferred_element_type=f32))

    hg = hg_ref[...]
    o_parts = []
    for hd in range(N_HEADS):
        sl = slice(hd * HEAD_DIM, (hd + 1) * HEAD_DIM)
        oh = o_s[:, sl]
        oms = jnp.mean(oh * oh, axis=-1, keepdims=True)
        o_parts.append(oh * lax.rsqrt(oms + EPS) * hg[:, sl])
    o_a = (jnp.concatenate(o_parts, axis=-1) * _silu(proj(3, D_HGRN))).astype(bf16)

    cw = cw_ref[...]
    cu = proj(6, D_CONV) * proj(4, D_CONV)
    cu_ref[SUBLANES:tile + SUBLANES, :] = cu
    conv = (cw[2:3, :] * cu
            + cw[1:2, :] * cu_ref[SUBLANES - 1:tile + SUBLANES - 1, :]
            + cw[0:1, :] * cu_ref[SUBLANES - 2:tile + SUBLANES - 2, :])
    cu_ref[0:SUBLANES, :] = cu_ref[tile:tile + SUBLANES, :]
    y = proj(5, D_CONV) * conv
    cg = cg_ref[...]
    lane = lax.broadcasted_iota(jnp.int32, (1, LANES), 1)
    first_group = lane < CONV_GROUP_DIM
    y_parts = []
    for j in range(D_CONV // LANES):
        sl = slice(j * LANES, (j + 1) * LANES)
        yj = y[:, sl]
        y2 = yj * yj
        tot = jnp.sum(y2, axis=-1, keepdims=True)
        lo = jnp.sum(jnp.where(first_group, y2, 0.0), axis=-1, keepdims=True)
        yms = jnp.where(first_group, lo, tot - lo) * (1.0 / CONV_GROUP_DIM)
        y_parts.append(yj * lax.rsqrt(yms + EPS) * cg[:, sl])
    o_b = (jnp.concatenate(y_parts, axis=-1) * _silu(proj(7, D_CONV))).astype(bf16)

    mix = (jnp.dot(o_a, wout_ref[0:D_HGRN, :], preferred_element_type=f32)
           + jnp.dot(o_b, wout_ref[D_HGRN:D_HGRN + D_CONV, :], preferred_element_type=f32))
    res = x + mix
    rms = jnp.mean(res * res, axis=-1, keepdims=True)
    out_ref[...] = res * lax.rsqrt(rms + EPS) * fg_ref[...]


@jax.jit
def kernel(x, norm_gain, w_in, lb_logits, conv_w, hgrn_norm_gain, conv_norm_gain, w_out,
           final_norm_gain):
    bsz, seq, d_model = x.shape
    depth = norm_gain.shape[0]
    assert depth == 1 and lb_logits.shape == (depth + 1, D_HGRN)
    assert w_in.shape == (depth, d_model, 4 * D_HGRN + 4 * D_CONV)
    assert w_out.shape == (depth, D_HGRN + D_CONV, d_model)
    tile = min(SEQ_TILE, seq)
    assert seq % tile == 0 and tile % CHUNK == 0

    full = lambda shape: pl.BlockSpec(shape, lambda b, s: (0,) * len(shape))
    in_cols = w_in.shape[2]
    grid_spec = pltpu.PrefetchScalarGridSpec(
        num_scalar_prefetch=0,
        grid=(bsz, seq // tile),
        in_specs=[
            pl.BlockSpec((None, tile, d_model), lambda b, s: (b, s, 0)),
            full((1, d_model)),
            full((d_model, in_cols)),
            full((depth + 1, D_HGRN)),
            full((CONV_WIDTH, D_CONV)),
            full((1, D_HGRN)),
            full((1, D_CONV)),
            full((D_HGRN + D_CONV, d_model)),
            full((1, d_model)),
        ],
        out_specs=pl.BlockSpec((None, tile, d_model), lambda b, s: (b, s, 0)),
        scratch_shapes=[
            pltpu.VMEM((N_HEADS, HEAD_DIM, HEAD_DIM), jnp.float32),
            pltpu.VMEM((tile + SUBLANES, D_CONV), jnp.float32),
            pltpu.VMEM((tile, D_HGRN), jnp.float32),
        ],
    )
    return pl.pallas_call(
        _block_kernel,
        grid_spec=grid_spec,
        out_shape=jax.ShapeDtypeStruct(x.shape, x.dtype),
        compiler_params=pltpu.CompilerParams(
            dimension_semantics=("arbitrary", "arbitrary"),
            vmem_limit_bytes=VMEM_LIMIT_BYTES),
        name="hgrn2_shortconv_block",
    )(x, norm_gain, w_in[0].astype(jnp.bfloat16), lb_logits, conv_w[0], hgrn_norm_gain,
      conv_norm_gain, w_out[0].astype(jnp.bfloat16), final_norm_gain.reshape(1, d_model))
```

```python
import jax
import jax.numpy as jnp
from jax import lax
from jax.experimental import pallas as pl
from jax.experimental.pallas import tpu as pltpu

D_HGRN = 512
D_CONV = 512
HEAD_DIM = 128
N_HEADS = D_HGRN // HEAD_DIM
CHUNK = 64
CONV_WIDTH = 3
CONV_GROUP_DIM = 64
EPS = 1e-6
SEQ_TILE = 1024
SUB_TILE = 512
LANES = 128
SUBLANES = 8
VMEM_LIMIT_BYTES = 60000 * 1024

_NT = (((1,), (1,)), ((), ()))
_TN = (((0,), (0,)), ((), ()))


def _silu(z):
    return z * jax.nn.sigmoid(z)


def _slab(x, r0, lb, g_ref, win_ref, cw_ref, hg_ref, cg_ref, wout_ref, fg_ref,
          st_ref, cu_ref, o_s):
    f32, bf16 = jnp.float32, jnp.bfloat16
    n_rows = x.shape[0]

    ms = jnp.mean(x * x, axis=-1, keepdims=True)
    h = (x * lax.rsqrt(ms + EPS) * g_ref[...]).astype(bf16)

    def proj(j, width):
        return jnp.dot(h, win_ref[:, j * width:(j + 1) * width], preferred_element_type=f32)

    q = proj(0, D_HGRN)
    f = lb + (1.0 - lb) * jax.nn.sigmoid(proj(1, D_HGRN))
    lf = jnp.log(f)
    k = 1.0 - f
    v = proj(2, D_HGRN).astype(bf16)

    row = lax.broadcasted_iota(jnp.int32, (CHUNK, CHUNK), 0)
    col = lax.broadcasted_iota(jnp.int32, (CHUNK, CHUNK), 1)
    causal = row >= col
    tri = causal.astype(bf16)
    n_chunks = n_rows // CHUNK
    rows = [slice(c * CHUNK, (c + 1) * CHUNK) for c in range(n_chunks)]
    heads = [slice(hd * HEAD_DIM, (hd + 1) * HEAD_DIM) for hd in range(N_HEADS)]

    lf_hi = lf.astype(bf16)
    lf_lo = (lf - lf_hi.astype(f32)).astype(bf16)
    b = jnp.concatenate(
        [jnp.dot(tri, lf_hi[r], preferred_element_type=f32)
         + jnp.dot(tri, lf_lo[r], preferred_element_type=f32) for r in rows], axis=0)
    eb = jnp.exp(b)
    q_dec = (q * eb).astype(bf16)
    k_inv = k * jnp.exp(-b)
    decay = [eb[r][CHUNK - 1:CHUNK, :] for r in rows]
    k_end = [(k_inv[r] * decay[c]).astype(bf16) for c, r in enumerate(rows)]
    k_inv = k_inv.astype(bf16)

    scores = [[lax.dot_general(q_dec[r, sl], k_inv[r, sl], _NT, preferred_element_type=f32)
               for sl in heads] for r in rows]
    upd_t = [[lax.dot_general(v[r, sl], k_end[c][:, sl], _TN, preferred_element_type=f32)
              for sl in heads] for c, r in enumerate(rows)]
    state_before = [[None] * N_HEADS for _ in rows]
    for hd, sl in enumerate(heads):
        state_t = st_ref[hd]
        for c in range(n_chunks):
            state_before[c][hd] = state_t.T.astype(bf16)
            state_t = state_t * decay[c][:, sl] + upd_t[c][hd]
        st_ref[hd] = state_t
    for c, r in enumerate(rows):
        for hd, sl in enumerate(heads):
            sc = jnp.where(causal, scores[c][hd], 0.0).astype(bf16)
            o_s[r0 + c * CHUNK:r0 + (c + 1) * CHUNK, sl] = (
                jnp.dot(sc, v[r, sl], preferred_element_type=f32)
                + jnp.dot(q_dec[r, sl], state_before[c][hd], preferred_element_type=f32))

    hg = hg_ref[...]
    o_parts = []
    for sl in heads:
        oh = o_s[r0:r0 + n_rows, sl]
        oms = jnp.mean(oh * oh, axis=-1, keepdims=True)
        o_parts.append(oh * lax.rsqrt(oms + EPS) * hg[:, sl])
    o_a = (jnp.concatenate(o_parts, axis=-1) * _silu(proj(3, D_HGRN))).astype(bf16)

    cw = cw_ref[...]
    cu = proj(6, D_CONV) * proj(4, D_CONV)
    c0 = r0 + SUBLANES
    cu_ref[c0:c0 + n_rows, :] = cu
    conv = (cw[2:3, :] * cu
            + cw[1:2, :] * cu_ref[c0 - 1:c0 - 1 + n_rows, :]
            + cw[0:1, :] * cu_ref[c0 - 2:c0 - 2 + n_rows, :])
    y = proj(5, D_CONV) * conv
    cg = cg_ref[...]
    lane = lax.broadcasted_iota(jnp.int32, (1, LANES), 1)
    first_group = lane < CONV_GROUP_DIM
    y_parts = []
    for j in range(D_CONV // LANES):
        sl = slice(j * LANES, (j + 1) * LANES)
        yj = y[:, sl]
        y2 = yj * yj
        tot = jnp.sum(y2, axis=-1, keepdims=True)
        lo = jnp.sum(jnp.where(first_group, y2, 0.0), axis=-1, keepdims=True)
        yms = jnp.where(first_group, lo, tot - lo) * (1.0 / CONV_GROUP_DIM)
        y_parts.append(yj * lax.rsqrt(yms + EPS) * cg[:, sl])
    o_b = (jnp.concatenate(y_parts, axis=-1) * _silu(proj(7, D_CONV))).astype(bf16)

    mix = (jnp.dot(o_a, wout_ref[0:D_HGRN, :], preferred_element_type=f32)
           + jnp.dot(o_b, wout_ref[D_HGRN:D_HGRN + D_CONV, :], preferred_element_type=f32))
    res = x + mix
    rms = jnp.mean(res * res, axis=-1, keepdims=True)
    return res * lax.rsqrt(rms + EPS) * fg_ref[...]


def _block_kernel(x_ref, g_ref, win_ref, lbl_ref, cw_ref, hg_ref, cg_ref, wout_ref, fg_ref,
                  out_ref, st_ref, cu_ref, o_s):
    tile = x_ref.shape[0]
    sub = min(SUB_TILE, tile)

    @pl.when(pl.program_id(1) == 0)
    def _reset_carries():
        st_ref[...] = jnp.zeros_like(st_ref)
        cu_ref[0:SUBLANES, :] = jnp.zeros((SUBLANES, D_CONV), jnp.float32)

    lbl = lbl_ref[...]
    lmax = jnp.max(lbl, axis=0, keepdims=True)
    lexp = jnp.exp(lbl - lmax)
    lb = lexp[0:1, :] / jnp.sum(lexp, axis=0, keepdims=True)

    for r0 in range(0, tile, sub):
        out_ref[r0:r0 + sub, :] = _slab(
            x_ref[r0:r0 + sub, :], r0, lb, g_ref, win_ref, cw_ref, hg_ref, cg_ref, wout_ref,
            fg_ref, st_ref, cu_ref, o_s)
    cu_ref[0:SUBLANES, :] = cu_ref[tile:tile + SUBLANES, :]


@jax.jit
def kernel(x, norm_gain, w_in, lb_logits, conv_w, hgrn_norm_gain, conv_norm_gain, w_out,
           final_norm_gain):
    bsz, seq, d_model = x.shape
    depth = norm_gain.shape[0]
    assert depth == 1 and lb_logits.shape == (depth + 1, D_HGRN)
    assert w_in.shape == (depth, d_model, 4 * D_HGRN + 4 * D_CONV)
    assert w_out.shape == (depth, D_HGRN + D_CONV, d_model)
    tile = min(SEQ_TILE, seq)
    sub = min(SUB_TILE, tile)
    assert seq % tile == 0 and tile % sub == 0 and sub % CHUNK == 0

    full = lambda shape: pl.BlockSpec(shape, lambda b, s: (0,) * len(shape))
    in_cols = w_in.shape[2]
    grid_spec = pltpu.PrefetchScalarGridSpec(
        num_scalar_prefetch=0,
        grid=(bsz, seq // tile),
        in_specs=[
            pl.BlockSpec((None, tile, d_model), lambda b, s: (b, s, 0)),
            full((1, d_model)),
            full((d_model, in_cols)),
            full((depth + 1, D_HGRN)),
            full((CONV_WIDTH, D_CONV)),
            full((1, D_HGRN)),
            full((1, D_CONV)),
            full((D_HGRN + D_CONV, d_model)),
            full((1, d_model)),
        ],
        out_specs=pl.BlockSpec((None, tile, d_model), lambda b, s: (b, s, 0)),
        scratch_shapes=[
            pltpu.VMEM((N_HEADS, HEAD_DIM, HEAD_DIM), jnp.float32),
            pltpu.VMEM((tile + SUBLANES, D_CONV), jnp.float32),
            pltpu.VMEM((tile, D_HGRN), jnp.float32),
        ],
    )
    return pl.pallas_call(
        _block_kernel,
        grid_spec=grid_spec,
        out_shape=jax.ShapeDtypeStruct(x.shape, x.dtype),
        compiler_params=pltpu.CompilerParams(
            dimension_semantics=("arbitrary", "arbitrary"),
            vmem_limit_bytes=VMEM_LIMIT_BYTES),
        name="hgrn2_shortconv_block",
    )(x, norm_gain, w_in[0].astype(jnp.bfloat16), lb_logits, conv_w[0], hgrn_norm_gain,
      conv_norm_gain, w_out[0].astype(jnp.bfloat16), final_norm_gain.reshape(1, d_model))
```

```python
import jax
import jax.numpy as jnp
from jax import lax
from jax.experimental import pallas as pl
from jax.experimental.pallas import tpu as pltpu

D_HGRN = 512
D_CONV = 512
HEAD_DIM = 128
N_HEADS = D_HGRN // HEAD_DIM
CHUNK = 64
CONV_WIDTH = 3
CONV_GROUP_DIM = 64
EPS = 1e-6
SEQ_TILE = 1024
SUB_TILE = 256
LANES = 128
SUBLANES = 8
VMEM_LIMIT_BYTES = 60000 * 1024

_NT = (((1,), (1,)), ((), ()))
_TN = (((0,), (0,)), ((), ()))


def _silu(z):
    return z * jax.nn.sigmoid(z)


def _stage_in(x, g_ref, win_ref):
    bf16 = jnp.bfloat16
    ms = jnp.mean(x * x, axis=-1, keepdims=True)
    h = (x * lax.rsqrt(ms + EPS) * g_ref[...]).astype(bf16)

    def proj(j, width):
        return jnp.dot(h, win_ref[:, j * width:(j + 1) * width],
                       preferred_element_type=jnp.float32)

    q = proj(0, D_HGRN)
    f_logit = proj(1, D_HGRN)
    v = proj(2, D_HGRN).astype(bf16)
    u = proj(4, D_CONV)
    return proj, q, f_logit, v, u


def _stage_mix(r0, lb, stage_in, st_ref, o_s):
    f32, bf16 = jnp.float32, jnp.bfloat16
    proj, q, f_logit, v, u = stage_in
    n_rows = q.shape[0]
    row = lax.broadcasted_iota(jnp.int32, (CHUNK, CHUNK), 0)
    col = lax.broadcasted_iota(jnp.int32, (CHUNK, CHUNK), 1)
    causal = row >= col
    tri = causal.astype(bf16)
    n_chunks = n_rows // CHUNK
    rows = [slice(c * CHUNK, (c + 1) * CHUNK) for c in range(n_chunks)]
    heads = [slice(hd * HEAD_DIM, (hd + 1) * HEAD_DIM) for hd in range(N_HEADS)]

    f = lb + (1.0 - lb) * jax.nn.sigmoid(f_logit)
    lf = jnp.log(f)
    k = 1.0 - f

    lf_hi = lf.astype(bf16)
    lf_lo = (lf - lf_hi.astype(f32)).astype(bf16)
    tri2 = jnp.concatenate([tri, tri], axis=1)
    b = jnp.concatenate(
        [jnp.dot(tri2, jnp.concatenate([lf_hi[r], lf_lo[r]], axis=0),
                 preferred_element_type=f32) for r in rows], axis=0)
    gate_c = proj(6, D_CONV)
    eb = jnp.exp(b)
    q_dec = (q * eb).astype(bf16)
    k_inv = k * jnp.exp(-b)
    decay = [eb[r][CHUNK - 1:CHUNK, :] for r in rows]
    k_end = [(k_inv[r] * decay[c]).astype(bf16) for c, r in enumerate(rows)]
    k_inv = k_inv.astype(bf16)

    scores = [[lax.dot_general(q_dec[r, sl], k_inv[r, sl], _NT, preferred_element_type=f32)
               for sl in heads] for r in rows]
    upd_t = [[lax.dot_general(v[r, sl], k_end[c][:, sl], _TN, preferred_element_type=f32)
              for sl in heads] for c, r in enumerate(rows)]
    gate_b = proj(5, D_CONV)
    state_before = [[None] * N_HEADS for _ in rows]
    for hd, sl in enumerate(heads):
        state_t = st_ref[hd]
        for c in range(n_chunks):
            state_before[c][hd] = state_t.T.astype(bf16)
            state_t = state_t * decay[c][:, sl] + upd_t[c][hd]
        st_ref[hd] = state_t
    for c, r in enumerate(rows):
        for hd, sl in enumerate(heads):
            sc = jnp.where(causal, scores[c][hd], 0.0).astype(bf16)
            o_s[r0 + c * CHUNK:r0 + (c + 1) * CHUNK, sl] = (
                jnp.dot(sc, v[r, sl], preferred_element_type=f32)
                + jnp.dot(q_dec[r, sl], state_before[c][hd], preferred_element_type=f32))
    z_b = proj(7, D_CONV)
    z_a = proj(3, D_HGRN)
    return gate_c * u, gate_b, z_a, z_b


def _stage_out(x, r0, stage_mix, cw_ref, hg_ref, cg_ref, wout_ref, fg_ref, cu_ref, o_s):
    bf16 = jnp.bfloat16
    cu, gate_b, z_a, z_b = stage_mix
    n_rows = x.shape[0]

    hg = hg_ref[...]
    o_parts = []
    for hd in range(N_HEADS):
        sl = slice(hd * HEAD_DIM, (hd + 1) * HEAD_DIM)
        oh = o_s[r0:r0 + n_rows, sl]
        oms = jnp.mean(oh * oh, axis=-1, keepdims=True)
        o_parts.append(oh * lax.rsqrt(oms + EPS) * hg[:, sl])
    o_a = (jnp.concatenate(o_parts, axis=-1) * _silu(z_a)).astype(bf16)

    cw = cw_ref[...]
    c0 = r0 + SUBLANES
    cu_ref[c0:c0 + n_rows, :] = cu
    conv = (cw[2:3, :] * cu
            + cw[1:2, :] * cu_ref[c0 - 1:c0 - 1 + n_rows, :]
            + cw[0:1, :] * cu_ref[c0 - 2:c0 - 2 + n_rows, :])
    y = gate_b * conv
    cg = cg_ref[...]
    lane = lax.broadcasted_iota(jnp.int32, (1, LANES), 1)
    first_group = lane < CONV_GROUP_DIM
    y_parts = []
    for j in range(D_CONV // LANES):
        sl = slice(j * LANES, (j + 1) * LANES)
        yj = y[:, sl]
        y2 = yj * yj
        tot = jnp.sum(y2, axis=-1, keepdims=True)
        lo = jnp.sum(jnp.where(first_group, y2, 0.0), axis=-1, keepdims=True)
        yms = jnp.where(first_group, lo, tot - lo) * (1.0 / CONV_GROUP_DIM)
        y_parts.append(yj * lax.rsqrt(yms + EPS) * cg[:, sl])
    o_b = (jnp.concatenate(y_parts, axis=-1) * _silu(z_b)).astype(bf16)

    mix = (jnp.dot(o_a, wout_ref[0:D_HGRN, :], preferred_element_type=jnp.float32)
           + jnp.dot(o_b, wout_ref[D_HGRN:D_HGRN + D_CONV, :],
                     preferred_element_type=jnp.float32))
    res = x + mix
    rms = jnp.mean(res * res, axis=-1, keepdims=True)
    return res * lax.rsqrt(rms + EPS) * fg_ref[...]


def _block_kernel(x_ref, g_ref, win_ref, lbl_ref, cw_ref, hg_ref, cg_ref, wout_ref, fg_ref,
                  out_ref, st_ref, cu_ref, o_s):
    tile = x_ref.shape[0]
    sub = min(SUB_TILE, tile)

    @pl.when(pl.program_id(1) == 0)
    def _reset_carries():
        st_ref[...] = jnp.zeros_like(st_ref)
        cu_ref[0:SUBLANES, :] = jnp.zeros((SUBLANES, D_CONV), jnp.float32)

    lbl = lbl_ref[...]
    lmax = jnp.max(lbl, axis=0, keepdims=True)
    lexp = jnp.exp(lbl - lmax)
    lb = lexp[0:1, :] / jnp.sum(lexp, axis=0, keepdims=True)

    starts = list(range(0, tile, sub))
    xs = [x_ref[r0:r0 + sub, :] for r0 in starts]
    stage_in = _stage_in(xs[0], g_ref, win_ref)
    for i, r0 in enumerate(starts):
        stage_mix = _stage_mix(r0, lb, stage_in, st_ref, o_s)
        if i + 1 < len(starts):
            stage_in = _stage_in(xs[i + 1], g_ref, win_ref)
        out_ref[r0:r0 + sub, :] = _stage_out(xs[i], r0, stage_mix, cw_ref, hg_ref, cg_ref,
                                             wout_ref, fg_ref, cu_ref, o_s)
    cu_ref[0:SUBLANES, :] = cu_ref[tile:tile + SUBLANES, :]


@jax.jit
def kernel(x, norm_gain, w_in, lb_logits, conv_w, hgrn_norm_gain, conv_norm_gain, w_out,
           final_norm_gain):
    bsz, seq, d_model = x.shape
    depth = norm_gain.shape[0]
    assert depth == 1 and lb_logits.shape == (depth + 1, D_HGRN)
    assert w_in.shape == (depth, d_model, 4 * D_HGRN + 4 * D_CONV)
    assert w_out.shape == (depth, D_HGRN + D_CONV, d_model)
    tile = min(SEQ_TILE, seq)
    sub = min(SUB_TILE, tile)
    assert seq % tile == 0 and tile % sub == 0 and sub % CHUNK == 0

    full = lambda shape: pl.BlockSpec(shape, lambda b, s: (0,) * len(shape))
    in_cols = w_in.shape[2]
    grid_spec = pltpu.PrefetchScalarGridSpec(
        num_scalar_prefetch=0,
        grid=(bsz, seq // tile),
        in_specs=[
            pl.BlockSpec((None, tile, d_model), lambda b, s: (b, s, 0)),
            full((1, d_model)),
            full((d_model, in_cols)),
            full((depth + 1, D_HGRN)),
            full((CONV_WIDTH, D_CONV)),
            full((1, D_HGRN)),
            full((1, D_CONV)),
            full((D_HGRN + D_CONV, d_model)),
            full((1, d_model)),
        ],
        out_specs=pl.BlockSpec((None, tile, d_model), lambda b, s: (b, s, 0)),
        scratch_shapes=[
            pltpu.VMEM((N_HEADS, HEAD_DIM, HEAD_DIM), jnp.float32),
            pltpu.VMEM((tile + SUBLANES, D_CONV), jnp.float32),
            pltpu.VMEM((tile, D_HGRN), jnp.float32),
        ],
    )
    return pl.pallas_call(
        _block_kernel,
        grid_spec=grid_spec,
        out_shape=jax.ShapeDtypeStruct(x.shape, x.dtype),
        compiler_params=pltpu.CompilerParams(
            dimension_semantics=("arbitrary", "arbitrary"),
            vmem_limit_bytes=VMEM_LIMIT_BYTES),
        name="hgrn2_shortconv_block",
    )(x, norm_gain, w_in[0].astype(jnp.bfloat16), lb_logits, conv_w[0], hgrn_norm_gain,
      conv_norm_gain, w_out[0].astype(jnp.bfloat16), final_norm_gain.reshape(1, d_model))
```

```python
import jax
import jax.numpy as jnp
from jax import lax
from jax.experimental import pallas as pl
from jax.experimental.pallas import tpu as pltpu

D_HGRN = 512
D_CONV = 512
HEAD_DIM = 128
N_HEADS = D_HGRN // HEAD_DIM
CHUNK = 64
CONV_WIDTH = 3
CONV_GROUP_DIM = 64
EPS = 1e-6
SEQ_TILE = 1024
SUB_TILE = 512
LANES = 128
SUBLANES = 8
VMEM_LIMIT_BYTES = 60000 * 1024

_NT = (((1,), (1,)), ((), ()))
_TN = (((0,), (0,)), ((), ()))


def _silu(z):
    return z * jax.nn.sigmoid(z)


def _stage_in(x, g_ref, win_ref):
    bf16 = jnp.bfloat16
    ms = jnp.mean(x * x, axis=-1, keepdims=True)
    h = (x * lax.rsqrt(ms + EPS) * g_ref[...]).astype(bf16)

    def proj(j, width):
        return jnp.dot(h, win_ref[:, j * width:(j + 1) * width],
                       preferred_element_type=jnp.float32)

    q = proj(0, D_HGRN)
    f_logit = proj(1, D_HGRN)
    v = proj(2, D_HGRN).astype(bf16)
    u = proj(4, D_CONV)
    return proj, q, f_logit, v, u


def _stage_mix(r0, lb, stage_in, st_ref, o_s):
    f32, bf16 = jnp.float32, jnp.bfloat16
    proj, q, f_logit, v, u = stage_in
    n_rows = q.shape[0]
    row = lax.broadcasted_iota(jnp.int32, (CHUNK, CHUNK), 0)
    col = lax.broadcasted_iota(jnp.int32, (CHUNK, CHUNK), 1)
    causal = row >= col
    tri = causal.astype(bf16)
    n_chunks = n_rows // CHUNK
    rows = [slice(c * CHUNK, (c + 1) * CHUNK) for c in range(n_chunks)]
    heads = [slice(hd * HEAD_DIM, (hd + 1) * HEAD_DIM) for hd in range(N_HEADS)]

    f = lb + (1.0 - lb) * jax.nn.sigmoid(f_logit)
    lf = jnp.log(f)
    k = 1.0 - f

    lf_hi = lf.astype(bf16)
    lf_lo = (lf - lf_hi.astype(f32)).astype(bf16)
    tri2 = jnp.concatenate([tri, tri], axis=1)
    b = jnp.concatenate(
        [jnp.dot(tri2, jnp.concatenate([lf_hi[r], lf_lo[r]], axis=0),
                 preferred_element_type=f32) for r in rows], axis=0)
    gate_c = proj(6, D_CONV)
    eb = jnp.exp(b)
    q_dec = (q * eb).astype(bf16)
    k_inv = k * jnp.exp(-b)
    decay = [eb[r][CHUNK - 1:CHUNK, :] for r in rows]
    k_end = [(k_inv[r] * decay[c]).astype(bf16) for c, r in enumerate(rows)]
    k_inv = k_inv.astype(bf16)

    scores = [[lax.dot_general(q_dec[r, sl], k_inv[r, sl], _NT, preferred_element_type=f32)
               for sl in heads] for r in rows]
    upd_t = [[lax.dot_general(v[r, sl], k_end[c][:, sl], _TN, preferred_element_type=f32)
              for sl in heads] for c, r in enumerate(rows)]
    gate_b = proj(5, D_CONV)
    state_before = [[None] * N_HEADS for _ in rows]
    for hd, sl in enumerate(heads):
        state_t = st_ref[hd]
        for c in range(n_chunks):
            state_before[c][hd] = state_t.T.astype(bf16)
            state_t = state_t * decay[c][:, sl] + upd_t[c][hd]
        st_ref[hd] = state_t
    for c, r in enumerate(rows):
        for hd, sl in enumerate(heads):
            sc = jnp.where(causal, scores[c][hd], 0.0).astype(bf16)
            o_s[r0 + c * CHUNK:r0 + (c + 1) * CHUNK, sl] = (
                jnp.dot(sc, v[r, sl], preferred_element_type=f32)
                + jnp.dot(q_dec[r, sl], state_before[c][hd], preferred_element_type=f32))
    z_b = proj(7, D_CONV)
    z_a = proj(3, D_HGRN)
    return gate_c * u, gate_b, z_a, z_b


def _stage_out(x, r0, stage_mix, cw_ref, hg_ref, cg_ref, wout_ref, fg_ref, cu_ref, o_s):
    bf16 = jnp.bfloat16
    cu, gate_b, z_a, z_b = stage_mix
    n_rows = x.shape[0]

    hg = hg_ref[...]
    o_parts = []
    for hd in range(N_HEADS):
        sl = slice(hd * HEAD_DIM, (hd + 1) * HEAD_DIM)
        oh = o_s[r0:r0 + n_rows, sl]
        oms = jnp.mean(oh * oh, axis=-1, keepdims=True)
        o_parts.append(oh * lax.rsqrt(oms + EPS) * hg[:, sl])
    o_a = (jnp.concatenate(o_parts, axis=-1) * _silu(z_a)).astype(bf16)

    cw = cw_ref[...]
    c0 = r0 + SUBLANES
    cu_ref[c0:c0 + n_rows, :] = cu
    conv = (cw[2:3, :] * cu
            + cw[1:2, :] * cu_ref[c0 - 1:c0 - 1 + n_rows, :]
            + cw[0:1, :] * cu_ref[c0 - 2:c0 - 2 + n_rows, :])
    y = gate_b * conv
    cg = cg_ref[...]
    lane = lax.broadcasted_iota(jnp.int32, (1, LANES), 1)
    first_group = lane < CONV_GROUP_DIM
    y_parts = []
    for j in range(D_CONV // LANES):
        sl = slice(j * LANES, (j + 1) * LANES)
        yj = y[:, sl]
        y2 = yj * yj
        tot = jnp.sum(y2, axis=-1, keepdims=True)
        lo = jnp.sum(jnp.where(first_group, y2, 0.0), axis=-1, keepdims=True)
        yms = jnp.where(first_group, lo, tot - lo) * (1.0 / CONV_GROUP_DIM)
        y_parts.append(yj * lax.rsqrt(yms + EPS) * cg[:, sl])
    o_b = (jnp.concatenate(y_parts, axis=-1) * _silu(z_b)).astype(bf16)

    mix = (jnp.dot(o_a, wout_ref[0:D_HGRN, :], preferred_element_type=jnp.float32)
           + jnp.dot(o_b, wout_ref[D_HGRN:D_HGRN + D_CONV, :],
                     preferred_element_type=jnp.float32))
    res = x + mix
    rms = jnp.mean(res * res, axis=-1, keepdims=True)
    return res * lax.rsqrt(rms + EPS) * fg_ref[...]


def _block_kernel(x_ref, g_ref, win_ref, lbl_ref, cw_ref, hg_ref, cg_ref, wout_ref, fg_ref,
                  out_ref, st_ref, cu_ref, o_s):
    tile = x_ref.shape[0]
    sub = min(SUB_TILE, tile)

    @pl.when(pl.program_id(1) == 0)
    def _reset_carries():
        st_ref[...] = jnp.zeros_like(st_ref)
        cu_ref[0:SUBLANES, :] = jnp.zeros((SUBLANES, D_CONV), jnp.float32)

    lbl = lbl_ref[...]
    lmax = jnp.max(lbl, axis=0, keepdims=True)
    lexp = jnp.exp(lbl - lmax)
    lb = lexp[0:1, :] / jnp.sum(lexp, axis=0, keepdims=True)

    starts = list(range(0, tile, sub))
    xs = [x_ref[r0:r0 + sub, :] for r0 in starts]
    stage_in = _stage_in(xs[0], g_ref, win_ref)
    for i, r0 in enumerate(starts):
        stage_mix = _stage_mix(r0, lb, stage_in, st_ref, o_s)
        if i + 1 < len(starts):
            stage_in = _stage_in(xs[i + 1], g_ref, win_ref)
        out_ref[r0:r0 + sub, :] = _stage_out(xs[i], r0, stage_mix, cw_ref, hg_ref, cg_ref,
                                             wout_ref, fg_ref, cu_ref, o_s)
    cu_ref[0:SUBLANES, :] = cu_ref[tile:tile + SUBLANES, :]


@jax.jit
def kernel(x, norm_gain, w_in, lb_logits, conv_w, hgrn_norm_gain, conv_norm_gain, w_out,
           final_norm_gain):
    bsz, seq, d_model = x.shape
    depth = norm_gain.shape[0]
    assert depth == 1 and lb_logits.shape == (depth + 1, D_HGRN)
    assert w_in.shape == (depth, d_model, 4 * D_HGRN + 4 * D_CONV)
    assert w_out.shape == (depth, D_HGRN + D_CONV, d_model)
    tile = min(SEQ_TILE, seq)
    sub = min(SUB_TILE, tile)
    assert seq % tile == 0 and tile % sub == 0 and sub % CHUNK == 0

    full = lambda shape: pl.BlockSpec(shape, lambda b, s: (0,) * len(shape))
    in_cols = w_in.shape[2]
    grid_spec = pltpu.PrefetchScalarGridSpec(
        num_scalar_prefetch=0,
        grid=(bsz, seq // tile),
        in_specs=[
            pl.BlockSpec((None, tile, d_model), lambda b, s: (b, s, 0)),
            full((1, d_model)),
            full((d_model, in_cols)),
            full((depth + 1, D_HGRN)),
            full((CONV_WIDTH, D_CONV)),
            full((1, D_HGRN)),
            full((1, D_CONV)),
            full((D_HGRN + D_CONV, d_model)),
            full((1, d_model)),
        ],
        out_specs=pl.BlockSpec((None, tile, d_model), lambda b, s: (b, s, 0)),
        scratch_shapes=[
            pltpu.VMEM((N_HEADS, HEAD_DIM, HEAD_DIM), jnp.float32),
            pltpu.VMEM((tile + SUBLANES, D_CONV), jnp.float32),
            pltpu.VMEM((tile, D_HGRN), jnp.float32),
        ],
    )
    return pl.pallas_call(
        _block_kernel,
        grid_spec=grid_spec,
        out_shape=jax.ShapeDtypeStruct(x.shape, x.dtype),
        compiler_params=pltpu.CompilerParams(
            dimension_semantics=("arbitrary", "arbitrary"),
            vmem_limit_bytes=VMEM_LIMIT_BYTES),
        name="hgrn2_shortconv_block",
    )(x, norm_gain, w_in[0].astype(jnp.bfloat16), lb_logits, conv_w[0], hgrn_norm_gain,
      conv_norm_gain, w_out[0].astype(jnp.bfloat16), final_norm_gain.reshape(1, d_model))
```

```python
import jax
import jax.numpy as jnp
from jax import lax
from jax.experimental import pallas as pl
from jax.experimental.pallas import tpu as pltpu

D_HGRN = 512
D_CONV = 512
HEAD_DIM = 128
N_HEADS = D_HGRN // HEAD_DIM
CHUNK = 64
CONV_WIDTH = 3
CONV_GROUP_DIM = 64
EPS = 1e-6
SEQ_TILE = 1024
SUB_TILE = 512
LANES = 128
SUBLANES = 8
VMEM_LIMIT_BYTES = 60000 * 1024
WEIGHT_SLAB = 512

_NT = (((1,), (1,)), ((), ()))
_TN = (((0,), (0,)), ((), ()))


def _silu(z):
    return z * jax.nn.sigmoid(z)


def _slab(x, r0, lb, g_ref, win_ref, cw_ref, hg_ref, cg_ref, wout_ref, fg_ref,
          st_ref, cu_ref, o_s):
    f32, bf16 = jnp.float32, jnp.bfloat16
    n_rows = x.shape[0]

    ms = jnp.mean(x * x, axis=-1, keepdims=True)
    h = (x * lax.rsqrt(ms + EPS) * g_ref[...]).astype(bf16)

    def proj(j, width):
        return jnp.dot(h, win_ref[:, j * width:(j + 1) * width], preferred_element_type=f32)

    q = proj(0, D_HGRN)
    f = lb + (1.0 - lb) * jax.nn.sigmoid(proj(1, D_HGRN))
    lf = jnp.log(f)
    k = 1.0 - f
    v = proj(2, D_HGRN).astype(bf16)

    row = lax.broadcasted_iota(jnp.int32, (CHUNK, CHUNK), 0)
    col = lax.broadcasted_iota(jnp.int32, (CHUNK, CHUNK), 1)
    causal = row >= col
    tri = causal.astype(bf16)
    n_chunks = n_rows // CHUNK
    rows = [slice(c * CHUNK, (c + 1) * CHUNK) for c in range(n_chunks)]
    heads = [slice(hd * HEAD_DIM, (hd + 1) * HEAD_DIM) for hd in range(N_HEADS)]

    lf_hi = lf.astype(bf16)
    lf_lo = (lf - lf_hi.astype(f32)).astype(bf16)
    b = jnp.concatenate(
        [jnp.dot(tri, lf_hi[r], preferred_element_type=f32)
         + jnp.dot(tri, lf_lo[r], preferred_element_type=f32) for r in rows], axis=0)
    eb = jnp.exp(b)
    q_dec = (q * eb).astype(bf16)
    k_inv = k * jnp.exp(-b)
    decay = [eb[r][CHUNK - 1:CHUNK, :] for r in rows]
    k_end = [(k_inv[r] * decay[c]).astype(bf16) for c, r in enumerate(rows)]
    k_inv = k_inv.astype(bf16)

    scores = [[lax.dot_general(q_dec[r, sl], k_inv[r, sl], _NT, preferred_element_type=f32)
               for sl in heads] for r in rows]
    upd_t = [[lax.dot_general(v[r, sl], k_end[c][:, sl], _TN, preferred_element_type=f32)
              for sl in heads] for c, r in enumerate(rows)]
    state_before = [[None] * N_HEADS for _ in rows]
    for hd, sl in enumerate(heads):
        state_t = st_ref[hd]
        for c in range(n_chunks):
            state_before[c][hd] = state_t.T.astype(bf16)
            state_t = state_t * decay[c][:, sl] + upd_t[c][hd]
        st_ref[hd] = state_t
    for c, r in enumerate(rows):
        for hd, sl in enumerate(heads):
            sc = jnp.where(causal, scores[c][hd], 0.0).astype(bf16)
            o_s[r0 + c * CHUNK:r0 + (c + 1) * CHUNK, sl] = (
                jnp.dot(sc, v[r, sl], preferred_element_type=f32)
                + jnp.dot(q_dec[r, sl], state_before[c][hd], preferred_element_type=f32))

    hg = hg_ref[...]
    o_parts = []
    for sl in heads:
        oh = o_s[r0:r0 + n_rows, sl]
        oms = jnp.mean(oh * oh, axis=-1, keepdims=True)
        o_parts.append(oh * lax.rsqrt(oms + EPS) * hg[:, sl])
    o_a = (jnp.concatenate(o_parts, axis=-1) * _silu(proj(3, D_HGRN))).astype(bf16)

    cw = cw_ref[...]
    cu = proj(6, D_CONV) * proj(4, D_CONV)
    c0 = r0 + SUBLANES
    cu_ref[c0:c0 + n_rows, :] = cu
    conv = (cw[2:3, :] * cu
            + cw[1:2, :] * cu_ref[c0 - 1:c0 - 1 + n_rows, :]
            + cw[0:1, :] * cu_ref[c0 - 2:c0 - 2 + n_rows, :])
    y = proj(5, D_CONV) * conv
    cg = cg_ref[...]
    lane = lax.broadcasted_iota(jnp.int32, (1, LANES), 1)
    first_group = lane < CONV_GROUP_DIM
    y_parts = []
    for j in range(D_CONV // LANES):
        sl = slice(j * LANES, (j + 1) * LANES)
        yj = y[:, sl]
        y2 = yj * yj
        tot = jnp.sum(y2, axis=-1, keepdims=True)
        lo = jnp.sum(jnp.where(first_group, y2, 0.0), axis=-1, keepdims=True)
        yms = jnp.where(first_group, lo, tot - lo) * (1.0 / CONV_GROUP_DIM)
        y_parts.append(yj * lax.rsqrt(yms + EPS) * cg[:, sl])
    o_b = (jnp.concatenate(y_parts, axis=-1) * _silu(proj(7, D_CONV))).astype(bf16)

    mix = (jnp.dot(o_a, wout_ref[0:D_HGRN, :], preferred_element_type=f32)
           + jnp.dot(o_b, wout_ref[D_HGRN:D_HGRN + D_CONV, :], preferred_element_type=f32))
    res = x + mix
    rms = jnp.mean(res * res, axis=-1, keepdims=True)
    return res * lax.rsqrt(rms + EPS) * fg_ref[...]


def _load_weights_as_bf16(win_hbm, wout_hbm, win_ref, wout_ref, stage, sems):
    slabs = [(win_hbm.at[0, :, pl.ds(j * WEIGHT_SLAB, WEIGHT_SLAB)],
              win_ref.at[:, pl.ds(j * WEIGHT_SLAB, WEIGHT_SLAB)])
             for j in range(win_ref.shape[1] // WEIGHT_SLAB)]
    slabs += [(wout_hbm.at[0, :, pl.ds(j * WEIGHT_SLAB, WEIGHT_SLAB)],
               wout_ref.at[:, pl.ds(j * WEIGHT_SLAB, WEIGHT_SLAB)])
              for j in range(wout_ref.shape[1] // WEIGHT_SLAB)]

    def copy(i):
        return pltpu.make_async_copy(slabs[i][0], stage.at[i % 2], sems.at[i % 2])

    copy(0).start()
    for i, (_, dst) in enumerate(slabs):
        if i + 1 < len(slabs):
            copy(i + 1).start()
        copy(i).wait()
        dst[...] = stage[i % 2].astype(jnp.bfloat16)


def _block_kernel(x_ref, g_ref, win_hbm, lbl_ref, cw_ref, hg_ref, cg_ref, wout_hbm, fg_ref,
                  out_ref, st_ref, cu_ref, o_s, win_ref, wout_ref, stage, sems):
    tile = x_ref.shape[0]
    sub = min(SUB_TILE, tile)

    @pl.when((pl.program_id(0) == 0) & (pl.program_id(1) == 0))
    def _first_step():
        _load_weights_as_bf16(win_hbm, wout_hbm, win_ref, wout_ref, stage, sems)

    @pl.when(pl.program_id(1) == 0)
    def _reset_carries():
        st_ref[...] = jnp.zeros_like(st_ref)
        cu_ref[0:SUBLANES, :] = jnp.zeros((SUBLANES, D_CONV), jnp.float32)

    lbl = lbl_ref[...]
    lmax = jnp.max(lbl, axis=0, keepdims=True)
    lexp = jnp.exp(lbl - lmax)
    lb = lexp[0:1, :] / jnp.sum(lexp, axis=0, keepdims=True)

    for r0 in range(0, tile, sub):
        out_ref[r0:r0 + sub, :] = _slab(
            x_ref[r0:r0 + sub, :], r0, lb, g_ref, win_ref, cw_ref, hg_ref, cg_ref, wout_ref,
            fg_ref, st_ref, cu_ref, o_s)
    cu_ref[0:SUBLANES, :] = cu_ref[tile:tile + SUBLANES, :]


@jax.jit
def kernel(x, norm_gain, w_in, lb_logits, conv_w, hgrn_norm_gain, conv_norm_gain, w_out,
           final_norm_gain):
    bsz, seq, d_model = x.shape
    depth = norm_gain.shape[0]
    assert depth == 1 and lb_logits.shape == (depth + 1, D_HGRN)
    assert w_in.shape == (depth, d_model, 4 * D_HGRN + 4 * D_CONV)
    assert w_out.shape == (depth, D_HGRN + D_CONV, d_model)
    tile = min(SEQ_TILE, seq)
    sub = min(SUB_TILE, tile)
    assert seq % tile == 0 and tile % sub == 0 and sub % CHUNK == 0
    in_cols = w_in.shape[2]
    assert in_cols % WEIGHT_SLAB == 0 and d_model % WEIGHT_SLAB == 0

    full = lambda shape: pl.BlockSpec(shape, lambda b, s: (0,) * len(shape))
    grid_spec = pltpu.PrefetchScalarGridSpec(
        num_scalar_prefetch=0,
        grid=(bsz, seq // tile),
        in_specs=[
            pl.BlockSpec((None, tile, d_model), lambda b, s: (b, s, 0)),
            full((1, d_model)),
            pl.BlockSpec(memory_space=pltpu.HBM),
            full((depth + 1, D_HGRN)),
            pl.BlockSpec((None, CONV_WIDTH, D_CONV), lambda b, s: (0, 0, 0)),
            full((1, D_HGRN)),
            full((1, D_CONV)),
            pl.BlockSpec(memory_space=pltpu.HBM),
            full((1, d_model)),
        ],
        out_specs=pl.BlockSpec((None, tile, d_model), lambda b, s: (b, s, 0)),
        scratch_shapes=[
            pltpu.VMEM((N_HEADS, HEAD_DIM, HEAD_DIM), jnp.float32),
            pltpu.VMEM((tile + SUBLANES, D_CONV), jnp.float32),
            pltpu.VMEM((tile, D_HGRN), jnp.float32),
            pltpu.VMEM((d_model, in_cols), jnp.bfloat16),
            pltpu.VMEM((D_HGRN + D_CONV, d_model), jnp.bfloat16),
            pltpu.VMEM((2, d_model, WEIGHT_SLAB), jnp.float32),
            pltpu.SemaphoreType.DMA((2,)),
        ],
    )
    return pl.pallas_call(
        _block_kernel,
        grid_spec=grid_spec,
        out_shape=jax.ShapeDtypeStruct(x.shape, x.dtype),
        compiler_params=pltpu.CompilerParams(
            dimension_semantics=("arbitrary", "arbitrary"),
            vmem_limit_bytes=VMEM_LIMIT_BYTES),
        name="hgrn2_shortconv_block",
    )(x, norm_gain, w_in, lb_logits, conv_w, hgrn_norm_gain, conv_norm_gain, w_out,
      final_norm_gain.reshape(1, d_model))
```

```python
import jax
import jax.numpy as jnp
from jax import lax
from jax.experimental import pallas as pl
from jax.experimental.pallas import tpu as pltpu

D_HGRN = 512
D_CONV = 512
HEAD_DIM = 128
N_HEADS = D_HGRN // HEAD_DIM
CHUNK = 64
CONV_WIDTH = 3
CONV_GROUP_DIM = 64
EPS = 1e-6
SEQ_TILE = 1024
SUB_TILE = 512
LANES = 128
SUBLANES = 8
VMEM_LIMIT_BYTES = 60000 * 1024
WEIGHT_SLAB = 512

_NT = (((1,), (1,)), ((), ()))
_TN = (((0,), (0,)), ((), ()))


def _silu(z):
    return z * jax.nn.sigmoid(z)


def _slab(x, r0, lb, g_ref, win_ref, cw_ref, hg_ref, cg_ref, wout_ref, fg_ref,
          st_ref, cu_ref, o_s):
    f32, bf16 = jnp.float32, jnp.bfloat16
    n_rows = x.shape[0]

    ms = jnp.mean(x * x, axis=-1, keepdims=True)
    h = (x * lax.rsqrt(ms + EPS) * g_ref[...]).astype(bf16)

    def proj(j, width):
        return jnp.dot(h, win_ref[:, j * width:(j + 1) * width], preferred_element_type=f32)

    q = proj(0, D_HGRN)
    f = lb + (1.0 - lb) * jax.nn.sigmoid(proj(1, D_HGRN))
    lf = jnp.log(f)
    k = 1.0 - f
    v = proj(2, D_HGRN).astype(bf16)

    row = lax.broadcasted_iota(jnp.int32, (CHUNK, CHUNK), 0)
    col = lax.broadcasted_iota(jnp.int32, (CHUNK, CHUNK), 1)
    causal = row >= col
    tri = causal.astype(bf16)
    n_chunks = n_rows // CHUNK
    rows = [slice(c * CHUNK, (c + 1) * CHUNK) for c in range(n_chunks)]
    heads = [slice(hd * HEAD_DIM, (hd + 1) * HEAD_DIM) for hd in range(N_HEADS)]

    lf_hi = lf.astype(bf16)
    lf_lo = (lf - lf_hi.astype(f32)).astype(bf16)
    tri2 = jnp.concatenate([tri, tri], axis=1)
    b = jnp.concatenate(
        [jnp.dot(tri2, jnp.concatenate([lf_hi[r], lf_lo[r]], axis=0),
                 preferred_element_type=f32) for r in rows], axis=0)
    eb = jnp.exp(b)
    q_dec = (q * eb).astype(bf16)
    k_inv = k * jnp.exp(-b)
    decay = [eb[r][CHUNK - 1:CHUNK, :] for r in rows]
    k_end = [(k_inv[r] * decay[c]).astype(bf16) for c, r in enumerate(rows)]
    k_inv = k_inv.astype(bf16)

    scores = [[lax.dot_general(q_dec[r, sl], k_inv[r, sl], _NT, preferred_element_type=f32)
               for sl in heads] for r in rows]
    upd_t = [[lax.dot_general(v[r, sl], k_end[c][:, sl], _TN, preferred_element_type=f32)
              for sl in heads] for c, r in enumerate(rows)]
    state_before = [[None] * N_HEADS for _ in rows]
    for hd, sl in enumerate(heads):
        state_t = st_ref[hd]
        for c in range(n_chunks):
            state_before[c][hd] = state_t.T.astype(bf16)
            state_t = state_t * decay[c][:, sl] + upd_t[c][hd]
        st_ref[hd] = state_t
    for c, r in enumerate(rows):
        for hd, sl in enumerate(heads):
            sc = jnp.where(causal, scores[c][hd], 0.0).astype(bf16)
            o_s[r0 + c * CHUNK:r0 + (c + 1) * CHUNK, sl] = (
                jnp.dot(sc, v[r, sl], preferred_element_type=f32)
                + jnp.dot(q_dec[r, sl], state_before[c][hd], preferred_element_type=f32))

    hg = hg_ref[...]
    o_parts = []
    for sl in heads:
        oh = o_s[r0:r0 + n_rows, sl]
        oms = jnp.mean(oh * oh, axis=-1, keepdims=True)
        o_parts.append(oh * lax.rsqrt(oms + EPS) * hg[:, sl])
    o_a = (jnp.concatenate(o_parts, axis=-1) * _silu(proj(3, D_HGRN))).astype(bf16)

    cw = [cw_ref[j] for j in range(CONV_WIDTH)]
    cu = proj(6, D_CONV) * proj(4, D_CONV)
    c0 = r0 + SUBLANES
    cu_ref[c0:c0 + n_rows, :] = cu
    conv = (cw[2] * cu
            + cw[1] * cu_ref[c0 - 1:c0 - 1 + n_rows, :]
            + cw[0] * cu_ref[c0 - 2:c0 - 2 + n_rows, :])
    y = proj(5, D_CONV) * conv
    cg = cg_ref[...]
    lane = lax.broadcasted_iota(jnp.int32, (1, LANES), 1)
    first_group = lane < CONV_GROUP_DIM
    y_parts = []
    for j in range(D_CONV // LANES):
        sl = slice(j * LANES, (j + 1) * LANES)
        yj = y[:, sl]
        y2 = yj * yj
        tot = jnp.sum(y2, axis=-1, keepdims=True)
        lo = jnp.sum(jnp.where(first_group, y2, 0.0), axis=-1, keepdims=True)
        yms = jnp.where(first_group, lo, tot - lo) * (1.0 / CONV_GROUP_DIM)
        y_parts.append(yj * lax.rsqrt(yms + EPS) * cg[:, sl])
    o_b = (jnp.concatenate(y_parts, axis=-1) * _silu(proj(7, D_CONV))).astype(bf16)

    mix = (jnp.dot(o_a, wout_ref[0:D_HGRN, :], preferred_element_type=f32)
           + jnp.dot(o_b, wout_ref[D_HGRN:D_HGRN + D_CONV, :], preferred_element_type=f32))
    res = x + mix
    rms = jnp.mean(res * res, axis=-1, keepdims=True)
    return res * lax.rsqrt(rms + EPS) * fg_ref[...]


def _load_weights_as_bf16(win_hbm, wout_hbm, win_ref, wout_ref, stage, sems):
    slabs = [(win_hbm.at[0, :, pl.ds(j * WEIGHT_SLAB, WEIGHT_SLAB)],
              win_ref.at[:, pl.ds(j * WEIGHT_SLAB, WEIGHT_SLAB)])
             for j in range(win_ref.shape[1] // WEIGHT_SLAB)]
    slabs += [(wout_hbm.at[0, :, pl.ds(j * WEIGHT_SLAB, WEIGHT_SLAB)],
               wout_ref.at[:, pl.ds(j * WEIGHT_SLAB, WEIGHT_SLAB)])
              for j in range(wout_ref.shape[1] // WEIGHT_SLAB)]

    def copy(i):
        return pltpu.make_async_copy(slabs[i][0], stage.at[i % 2], sems.at[i % 2])

    copy(0).start()
    for i, (_, dst) in enumerate(slabs):
        if i + 1 < len(slabs):
            copy(i + 1).start()
        copy(i).wait()
        dst[...] = stage[i % 2].astype(jnp.bfloat16)


def _block_kernel(x_ref, g_ref, win_hbm, lbl_ref, cw_ref, hg_ref, cg_ref, wout_hbm, fg_ref,
                  out_ref, st_ref, cu_ref, o_s, win_ref, wout_ref, stage, sems):
    tile = x_ref.shape[0]
    sub = min(SUB_TILE, tile)

    @pl.when((pl.program_id(0) == 0) & (pl.program_id(1) == 0))
    def _first_step():
        _load_weights_as_bf16(win_hbm, wout_hbm, win_ref, wout_ref, stage, sems)

    @pl.when(pl.program_id(1) == 0)
    def _reset_carries():
        st_ref[...] = jnp.zeros_like(st_ref)
        cu_ref[0:SUBLANES, :] = jnp.zeros((SUBLANES, D_CONV), jnp.float32)

    lbl = lbl_ref[...]
    lmax = jnp.max(lbl, axis=0, keepdims=True)
    lexp = jnp.exp(lbl - lmax)
    lb = lexp[0:1, :] / jnp.sum(lexp, axis=0, keepdims=True)

    for r0 in range(0, tile, sub):
        out_ref[r0:r0 + sub, :] = _slab(
            x_ref[r0:r0 + sub, :], r0, lb, g_ref, win_ref, cw_ref, hg_ref, cg_ref, wout_ref,
            fg_ref, st_ref, cu_ref, o_s)
    cu_ref[0:SUBLANES, :] = cu_ref[tile:tile + SUBLANES, :]


@jax.jit
def kernel(x, norm_gain, w_in, lb_logits, conv_w, hgrn_norm_gain, conv_norm_gain, w_out,
           final_norm_gain):
    bsz, seq, d_model = x.shape
    depth = norm_gain.shape[0]
    assert depth == 1 and lb_logits.shape == (depth + 1, D_HGRN)
    assert w_in.shape == (depth, d_model, 4 * D_HGRN + 4 * D_CONV)
    assert w_out.shape == (depth, D_HGRN + D_CONV, d_model)
    tile = min(SEQ_TILE, seq)
    sub = min(SUB_TILE, tile)
    assert seq % tile == 0 and tile % sub == 0 and sub % CHUNK == 0
    in_cols = w_in.shape[2]
    assert in_cols % WEIGHT_SLAB == 0 and d_model % WEIGHT_SLAB == 0

    full = lambda shape: pl.BlockSpec(shape, lambda b, s: (0,) * len(shape))
    grid_spec = pltpu.PrefetchScalarGridSpec(
        num_scalar_prefetch=0,
        grid=(bsz, seq // tile),
        in_specs=[
            pl.BlockSpec((None, tile, d_model), lambda b, s: (b, s, 0)),
            full((1, d_model)),
            pl.BlockSpec(memory_space=pltpu.HBM),
            full((depth + 1, D_HGRN)),
            full((CONV_WIDTH, 1, D_CONV)),
            full((1, D_HGRN)),
            full((1, D_CONV)),
            pl.BlockSpec(memory_space=pltpu.HBM),
            full((1, d_model)),
        ],
        out_specs=pl.BlockSpec((None, tile, d_model), lambda b, s: (b, s, 0)),
        scratch_shapes=[
            pltpu.VMEM((N_HEADS, HEAD_DIM, HEAD_DIM), jnp.float32),
            pltpu.VMEM((tile + SUBLANES, D_CONV), jnp.float32),
            pltpu.VMEM((tile, D_HGRN), jnp.float32),
            pltpu.VMEM((d_model, in_cols), jnp.bfloat16),
            pltpu.VMEM((D_HGRN + D_CONV, d_model), jnp.bfloat16),
            pltpu.VMEM((2, d_model, WEIGHT_SLAB), jnp.float32),
            pltpu.SemaphoreType.DMA((2,)),
        ],
    )
    return pl.pallas_call(
        _block_kernel,
        grid_spec=grid_spec,
        out_shape=jax.ShapeDtypeStruct(x.shape, x.dtype),
        compiler_params=pltpu.CompilerParams(
            dimension_semantics=("arbitrary", "arbitrary"),
            vmem_limit_bytes=VMEM_LIMIT_BYTES),
        name="hgrn2_shortconv_block",
    )(x, norm_gain, w_in, lb_logits, jnp.transpose(conv_w, (1, 0, 2)), hgrn_norm_gain,
      conv_norm_gain, w_out, final_norm_gain.reshape(1, d_model))
```

```python
import jax
import jax.numpy as jnp
from jax import lax
from jax.experimental import pallas as pl
from jax.experimental.pallas import tpu as pltpu

D_HGRN = 512
D_CONV = 512
HEAD_DIM = 128
N_HEADS = D_HGRN // HEAD_DIM
CHUNK = 64
CONV_WIDTH = 3
CONV_GROUP_DIM = 64
EPS = 1e-6
SEQ_TILE = 1024
SUB_TILE = 512
LANES = 128
SUBLANES = 8
VMEM_LIMIT_BYTES = 60000 * 1024
WEIGHT_SLAB = 512

_NT = (((1,), (1,)), ((), ()))
_TN = (((0,), (0,)), ((), ()))


def _silu(z):
    hz = 0.5 * z
    return hz + hz * jnp.tanh(hz)


def _slab(x, r0, lb, g_ref, win_ref, cw_ref, hg_ref, cg_ref, wout_ref, fg_ref,
          st_ref, cu_ref, o_s):
    f32, bf16 = jnp.float32, jnp.bfloat16
    n_rows = x.shape[0]

    ms = jnp.mean(x * x, axis=-1, keepdims=True)
    h = (x * lax.rsqrt(ms + EPS) * g_ref[...]).astype(bf16)

    def proj(j, width):
        return jnp.dot(h, win_ref[:, j * width:(j + 1) * width], preferred_element_type=f32)

    q = proj(0, D_HGRN)
    f = (0.5 + 0.5 * lb) + (0.5 - 0.5 * lb) * jnp.tanh(0.5 * proj(1, D_HGRN))
    lf = jnp.log(f)
    k = 1.0 - f
    v = proj(2, D_HGRN).astype(bf16)

    row = lax.broadcasted_iota(jnp.int32, (CHUNK, CHUNK), 0)
    col = lax.broadcasted_iota(jnp.int32, (CHUNK, CHUNK), 1)
    causal = row >= col
    tri = causal.astype(bf16)
    n_chunks = n_rows // CHUNK
    rows = [slice(c * CHUNK, (c + 1) * CHUNK) for c in range(n_chunks)]
    heads = [slice(hd * HEAD_DIM, (hd + 1) * HEAD_DIM) for hd in range(N_HEADS)]

    lf_hi = lf.astype(bf16)
    lf_lo = (lf - lf_hi.astype(f32)).astype(bf16)
    tri2 = jnp.concatenate([tri, tri], axis=1)
    b = jnp.concatenate(
        [jnp.dot(tri2, jnp.concatenate([lf_hi[r], lf_lo[r]], axis=0),
                 preferred_element_type=f32) for r in rows], axis=0)
    eb = jnp.exp(b)
    q_dec = (q * eb).astype(bf16)
    k_inv = k * jnp.exp(-b)
    decay = [eb[r][CHUNK - 1:CHUNK, :] for r in rows]
    k_end = [(k_inv[r] * decay[c]).astype(bf16) for c, r in enumerate(rows)]
    k_inv = k_inv.astype(bf16)

    scores = [[lax.dot_general(q_dec[r, sl], k_inv[r, sl], _NT, preferred_element_type=f32)
               for sl in heads] for r in rows]
    upd_t = [[lax.dot_general(v[r, sl], k_end[c][:, sl], _TN, preferred_element_type=f32)
              for sl in heads] for c, r in enumerate(rows)]
    state_before = [[None] * N_HEADS for _ in rows]
    for hd, sl in enumerate(heads):
        state_t = st_ref[hd]
        for c in range(n_chunks):
            state_before[c][hd] = state_t.T.astype(bf16)
            state_t = state_t * decay[c][:, sl] + upd_t[c][hd]
        st_ref[hd] = state_t
    for c, r in enumerate(rows):
        for hd, sl in enumerate(heads):
            sc = jnp.where(causal, scores[c][hd], 0.0).astype(bf16)
            o_s[r0 + c * CHUNK:r0 + (c + 1) * CHUNK, sl] = (
                jnp.dot(sc, v[r, sl], preferred_element_type=f32)
                + jnp.dot(q_dec[r, sl], state_before[c][hd], preferred_element_type=f32))

    hg = hg_ref[...]
    o_parts = []
    for sl in heads:
        oh = o_s[r0:r0 + n_rows, sl]
        oms = jnp.mean(oh * oh, axis=-1, keepdims=True)
        o_parts.append(oh * lax.rsqrt(oms + EPS) * hg[:, sl])
    o_a = (jnp.concatenate(o_parts, axis=-1) * _silu(proj(3, D_HGRN))).astype(bf16)

    cw = [cw_ref[j] for j in range(CONV_WIDTH)]
    cu = proj(6, D_CONV) * proj(4, D_CONV)
    c0 = r0 + SUBLANES
    cu_ref[c0:c0 + n_rows, :] = cu
    conv = (cw[2] * cu
            + cw[1] * cu_ref[c0 - 1:c0 - 1 + n_rows, :]
            + cw[0] * cu_ref[c0 - 2:c0 - 2 + n_rows, :])
    y = proj(5, D_CONV) * conv
    cg = cg_ref[...]
    lane = lax.broadcasted_iota(jnp.int32, (1, LANES), 1)
    first_group = lane < CONV_GROUP_DIM
    y_parts = []
    for j in range(D_CONV // LANES):
        sl = slice(j * LANES, (j + 1) * LANES)
        yj = y[:, sl]
        y2 = yj * yj
        tot = jnp.sum(y2, axis=-1, keepdims=True)
        lo = jnp.sum(jnp.where(first_group, y2, 0.0), axis=-1, keepdims=True)
        yms = jnp.where(first_group, lo, tot - lo) * (1.0 / CONV_GROUP_DIM)
        y_parts.append(yj * lax.rsqrt(yms + EPS) * cg[:, sl])
    o_b = (jnp.concatenate(y_parts, axis=-1) * _silu(proj(7, D_CONV))).astype(bf16)

    mix = (jnp.dot(o_a, wout_ref[0:D_HGRN, :], preferred_element_type=f32)
           + jnp.dot(o_b, wout_ref[D_HGRN:D_HGRN + D_CONV, :], preferred_element_type=f32))
    res = x + mix
    rms = jnp.mean(res * res, axis=-1, keepdims=True)
    return res * lax.rsqrt(rms + EPS) * fg_ref[...]


def _load_weights_as_bf16(win_hbm, wout_hbm, win_ref, wout_ref, stage, sems):
    slabs = [(win_hbm.at[0, :, pl.ds(j * WEIGHT_SLAB, WEIGHT_SLAB)],
              win_ref.at[:, pl.ds(j * WEIGHT_SLAB, WEIGHT_SLAB)])
             for j in range(win_ref.shape[1] // WEIGHT_SLAB)]
    slabs += [(wout_hbm.at[0, :, pl.ds(j * WEIGHT_SLAB, WEIGHT_SLAB)],
               wout_ref.at[:, pl.ds(j * WEIGHT_SLAB, WEIGHT_SLAB)])
              for j in range(wout_ref.shape[1] // WEIGHT_SLAB)]

    def copy(i):
        return pltpu.make_async_copy(slabs[i][0], stage.at[i % 2], sems.at[i % 2])

    copy(0).start()
    for i, (_, dst) in enumerate(slabs):
        if i + 1 < len(slabs):
            copy(i + 1).start()
        copy(i).wait()
        dst[...] = stage[i % 2].astype(jnp.bfloat16)


def _block_kernel(x_ref, g_ref, win_hbm, lbl_ref, cw_ref, hg_ref, cg_ref, wout_hbm, fg_ref,
                  out_ref, st_ref, cu_ref, o_s, win_ref, wout_ref, stage, sems):
    tile = x_ref.shape[0]
    sub = min(SUB_TILE, tile)

    @pl.when((pl.program_id(0) == 0) & (pl.program_id(1) == 0))
    def _first_step():
        _load_weights_as_bf16(win_hbm, wout_hbm, win_ref, wout_ref, stage, sems)

    @pl.when(pl.program_id(1) == 0)
    def _reset_carries():
        st_ref[...] = jnp.zeros_like(st_ref)
        cu_ref[0:SUBLANES, :] = jnp.zeros((SUBLANES, D_CONV), jnp.float32)

    lbl = lbl_ref[...]
    lmax = jnp.max(lbl, axis=0, keepdims=True)
    lexp = jnp.exp(lbl - lmax)
    lb = lexp[0:1, :] / jnp.sum(lexp, axis=0, keepdims=True)

    for r0 in range(0, tile, sub):
        out_ref[r0:r0 + sub, :] = _slab(
            x_ref[r0:r0 + sub, :], r0, lb, g_ref, win_ref, cw_ref, hg_ref, cg_ref, wout_ref,
            fg_ref, st_ref, cu_ref, o_s)
    cu_ref[0:SUBLANES, :] = cu_ref[tile:tile + SUBLANES, :]


@jax.jit
def kernel(x, norm_gain, w_in, lb_logits, conv_w, hgrn_norm_gain, conv_norm_gain, w_out,
           final_norm_gain):
    bsz, seq, d_model = x.shape
    depth = norm_gain.shape[0]
    assert depth == 1 and lb_logits.shape == (depth + 1, D_HGRN)
    assert w_in.shape == (depth, d_model, 4 * D_HGRN + 4 * D_CONV)
    assert w_out.shape == (depth, D_HGRN + D_CONV, d_model)
    tile = min(SEQ_TILE, seq)
    sub = min(SUB_TILE, tile)
    assert seq % tile == 0 and tile % sub == 0 and sub % CHUNK == 0
    in_cols = w_in.shape[2]
    assert in_cols % WEIGHT_SLAB == 0 and d_model % WEIGHT_SLAB == 0

    full = lambda shape: pl.BlockSpec(shape, lambda b, s: (0,) * len(shape))
    grid_spec = pltpu.PrefetchScalarGridSpec(
        num_scalar_prefetch=0,
        grid=(bsz, seq // tile),
        in_specs=[
            pl.BlockSpec((None, tile, d_model), lambda b, s: (b, s, 0)),
            full((1, d_model)),
            pl.BlockSpec(memory_space=pltpu.HBM),
            full((depth + 1, D_HGRN)),
            full((CONV_WIDTH, 1, D_CONV)),
            full((1, D_HGRN)),
            full((1, D_CONV)),
            pl.BlockSpec(memory_space=pltpu.HBM),
            full((1, d_model)),
        ],
        out_specs=pl.BlockSpec((None, tile, d_model), lambda b, s: (b, s, 0)),
        scratch_shapes=[
            pltpu.VMEM((N_HEADS, HEAD_DIM, HEAD_DIM), jnp.float32),
            pltpu.VMEM((tile + SUBLANES, D_CONV), jnp.float32),
            pltpu.VMEM((tile, D_HGRN), jnp.float32),
            pltpu.VMEM((d_model, in_cols), jnp.bfloat16),
            pltpu.VMEM((D_HGRN + D_CONV, d_model), jnp.bfloat16),
            pltpu.VMEM((2, d_model, WEIGHT_SLAB), jnp.float32),
            pltpu.SemaphoreType.DMA((2,)),
        ],
    )
    return pl.pallas_call(
        _block_kernel,
        grid_spec=grid_spec,
        out_shape=jax.ShapeDtypeStruct(x.shape, x.dtype),
        compiler_params=pltpu.CompilerParams(
            dimension_semantics=("arbitrary", "arbitrary"),
            vmem_limit_bytes=VMEM_LIMIT_BYTES),
        name="hgrn2_shortconv_block",
    )(x, norm_gain, w_in, lb_logits, jnp.transpose(conv_w, (1, 0, 2)), hgrn_norm_gain,
      conv_norm_gain, w_out, final_norm_gain.reshape(1, d_model))
```

```python
import jax
import jax.numpy as jnp
from jax import lax
from jax.experimental import pallas as pl
from jax.experimental.pallas import tpu as pltpu

D_HGRN = 512
D_CONV = 512
HEAD_DIM = 128
N_HEADS = D_HGRN // HEAD_DIM
CHUNK = 64
CONV_WIDTH = 3
CONV_GROUP_DIM = 64
EPS = 1e-6
SEQ_TILE = 2048
SUB_TILE = 512
LANES = 128
SUBLANES = 8
VMEM_LIMIT_BYTES = 60000 * 1024
WEIGHT_SLAB = 256

_NT = (((1,), (1,)), ((), ()))
_TN = (((0,), (0,)), ((), ()))


def _silu(z):
    hz = 0.5 * z
    return hz + hz * jnp.tanh(hz)


def _slab(x, r0, lb, g_ref, win_ref, cw_ref, hg_ref, cg_ref, wout_ref, fg_ref,
          st_ref, cu_ref, o_s):
    f32, bf16 = jnp.float32, jnp.bfloat16
    n_rows = x.shape[0]

    ms = jnp.mean(x * x, axis=-1, keepdims=True)
    h = (x * lax.rsqrt(ms + EPS) * g_ref[...]).astype(bf16)

    def proj(j, width):
        return jnp.dot(h, win_ref[:, j * width:(j + 1) * width], preferred_element_type=f32)

    q = proj(0, D_HGRN)
    f = (0.5 + 0.5 * lb) + (0.5 - 0.5 * lb) * jnp.tanh(0.5 * proj(1, D_HGRN))
    lf = jnp.log(f)
    k = 1.0 - f
    v = proj(2, D_HGRN).astype(bf16)

    row = lax.broadcasted_iota(jnp.int32, (CHUNK, CHUNK), 0)
    col = lax.broadcasted_iota(jnp.int32, (CHUNK, CHUNK), 1)
    causal = row >= col
    tri = causal.astype(bf16)
    n_chunks = n_rows // CHUNK
    rows = [slice(c * CHUNK, (c + 1) * CHUNK) for c in range(n_chunks)]
    heads = [slice(hd * HEAD_DIM, (hd + 1) * HEAD_DIM) for hd in range(N_HEADS)]

    lf_hi = lf.astype(bf16)
    lf_lo = (lf - lf_hi.astype(f32)).astype(bf16)
    tri2 = jnp.concatenate([tri, tri], axis=1)
    b = jnp.concatenate(
        [jnp.dot(tri2, jnp.concatenate([lf_hi[r], lf_lo[r]], axis=0),
                 preferred_element_type=f32) for r in rows], axis=0)
    eb = jnp.exp(b)
    q_dec = (q * eb).astype(bf16)
    k_inv = k * jnp.exp(-b)
    decay = [eb[r][CHUNK - 1:CHUNK, :] for r in rows]
    k_end = [(k_inv[r] * decay[c]).astype(bf16) for c, r in enumerate(rows)]
    k_inv = k_inv.astype(bf16)

    scores = [[lax.dot_general(q_dec[r, sl], k_inv[r, sl], _NT, preferred_element_type=f32)
               for sl in heads] for r in rows]
    upd_t = [[lax.dot_general(v[r, sl], k_end[c][:, sl], _TN, preferred_element_type=f32)
              for sl in heads] for c, r in enumerate(rows)]
    state_before = [[None] * N_HEADS for _ in rows]
    for hd, sl in enumerate(heads):
        state_t = st_ref[hd]
        for c in range(n_chunks):
            state_before[c][hd] = state_t.T.astype(bf16)
            state_t = state_t * decay[c][:, sl] + upd_t[c][hd]
        st_ref[hd] = state_t
    for c, r in enumerate(rows):
        for hd, sl in enumerate(heads):
            sc = jnp.where(causal, scores[c][hd], 0.0).astype(bf16)
            o_s[r0 + c * CHUNK:r0 + (c + 1) * CHUNK, sl] = (
                jnp.dot(sc, v[r, sl], preferred_element_type=f32)
                + jnp.dot(q_dec[r, sl], state_before[c][hd], preferred_element_type=f32))

    hg = hg_ref[...]
    o_parts = []
    for sl in heads:
        oh = o_s[r0:r0 + n_rows, sl]
        oms = jnp.mean(oh * oh, axis=-1, keepdims=True)
        o_parts.append(oh * lax.rsqrt(oms + EPS) * hg[:, sl])
    o_a = (jnp.concatenate(o_parts, axis=-1) * _silu(proj(3, D_HGRN))).astype(bf16)

    cw = [cw_ref[j] for j in range(CONV_WIDTH)]
    cu = proj(6, D_CONV) * proj(4, D_CONV)
    c0 = r0 + SUBLANES
    cu_ref[c0:c0 + n_rows, :] = cu
    conv = (cw[2] * cu
            + cw[1] * cu_ref[c0 - 1:c0 - 1 + n_rows, :]
            + cw[0] * cu_ref[c0 - 2:c0 - 2 + n_rows, :])
    y = proj(5, D_CONV) * conv
    cg = cg_ref[...]
    lane = lax.broadcasted_iota(jnp.int32, (1, LANES), 1)
    first_group = lane < CONV_GROUP_DIM
    y_parts = []
    for j in range(D_CONV // LANES):
        sl = slice(j * LANES, (j + 1) * LANES)
        yj = y[:, sl]
        y2 = yj * yj
        tot = jnp.sum(y2, axis=-1, keepdims=True)
        lo = jnp.sum(jnp.where(first_group, y2, 0.0), axis=-1, keepdims=True)
        yms = jnp.where(first_group, lo, tot - lo) * (1.0 / CONV_GROUP_DIM)
        y_parts.append(yj * lax.rsqrt(yms + EPS) * cg[:, sl])
    o_b = (jnp.concatenate(y_parts, axis=-1) * _silu(proj(7, D_CONV))).astype(bf16)

    mix = (jnp.dot(o_a, wout_ref[0:D_HGRN, :], preferred_element_type=f32)
           + jnp.dot(o_b, wout_ref[D_HGRN:D_HGRN + D_CONV, :], preferred_element_type=f32))
    res = x + mix
    rms = jnp.mean(res * res, axis=-1, keepdims=True)
    return res * lax.rsqrt(rms + EPS) * fg_ref[...]


def _load_weights_as_bf16(win_hbm, wout_hbm, win_ref, wout_ref, stage, sems):
    slabs = [(win_hbm.at[0, :, pl.ds(j * WEIGHT_SLAB, WEIGHT_SLAB)],
              win_ref.at[:, pl.ds(j * WEIGHT_SLAB, WEIGHT_SLAB)])
             for j in range(win_ref.shape[1] // WEIGHT_SLAB)]
    slabs += [(wout_hbm.at[0, :, pl.ds(j * WEIGHT_SLAB, WEIGHT_SLAB)],
               wout_ref.at[:, pl.ds(j * WEIGHT_SLAB, WEIGHT_SLAB)])
              for j in range(wout_ref.shape[1] // WEIGHT_SLAB)]

    def copy(i):
        return pltpu.make_async_copy(slabs[i][0], stage.at[i % 2], sems.at[i % 2])

    copy(0).start()
    for i, (_, dst) in enumerate(slabs):
        if i + 1 < len(slabs):
            copy(i + 1).start()
        copy(i).wait()
        dst[...] = stage[i % 2].astype(jnp.bfloat16)


def _block_kernel(x_ref, g_ref, win_hbm, lbl_ref, cw_ref, hg_ref, cg_ref, wout_hbm, fg_ref,
                  out_ref, st_ref, cu_ref, o_s, win_ref, wout_ref, stage, sems):
    tile = x_ref.shape[0]
    sub = min(SUB_TILE, tile)

    @pl.when((pl.program_id(0) == 0) & (pl.program_id(1) == 0))
    def _first_step():
        _load_weights_as_bf16(win_hbm, wout_hbm, win_ref, wout_ref, stage, sems)

    @pl.when(pl.program_id(1) == 0)
    def _reset_carries():
        st_ref[...] = jnp.zeros_like(st_ref)
        cu_ref[0:SUBLANES, :] = jnp.zeros((SUBLANES, D_CONV), jnp.float32)

    lbl = lbl_ref[...]
    lmax = jnp.max(lbl, axis=0, keepdims=True)
    lexp = jnp.exp(lbl - lmax)
    lb = lexp[0:1, :] / jnp.sum(lexp, axis=0, keepdims=True)

    for r0 in range(0, tile, sub):
        out_ref[r0:r0 + sub, :] = _slab(
            x_ref[r0:r0 + sub, :], 0, lb, g_ref, win_ref, cw_ref, hg_ref, cg_ref, wout_ref,
            fg_ref, st_ref, cu_ref, o_s)
        cu_ref[0:SUBLANES, :] = cu_ref[sub:sub + SUBLANES, :]


@jax.jit
def kernel(x, norm_gain, w_in, lb_logits, conv_w, hgrn_norm_gain, conv_norm_gain, w_out,
           final_norm_gain):
    bsz, seq, d_model = x.shape
    depth = norm_gain.shape[0]
    assert depth == 1 and lb_logits.shape == (depth + 1, D_HGRN)
    assert w_in.shape == (depth, d_model, 4 * D_HGRN + 4 * D_CONV)
    assert w_out.shape == (depth, D_HGRN + D_CONV, d_model)
    tile = min(SEQ_TILE, seq)
    sub = min(SUB_TILE, tile)
    assert seq % tile == 0 and tile % sub == 0 and sub % CHUNK == 0
    in_cols = w_in.shape[2]
    assert in_cols % WEIGHT_SLAB == 0 and d_model % WEIGHT_SLAB == 0

    full = lambda shape: pl.BlockSpec(shape, lambda b, s: (0,) * len(shape))
    grid_spec = pltpu.PrefetchScalarGridSpec(
        num_scalar_prefetch=0,
        grid=(bsz, seq // tile),
        in_specs=[
            pl.BlockSpec((None, tile, d_model), lambda b, s: (b, s, 0)),
            full((1, d_model)),
            pl.BlockSpec(memory_space=pltpu.HBM),
            full((depth + 1, D_HGRN)),
            full((CONV_WIDTH, 1, D_CONV)),
            full((1, D_HGRN)),
            full((1, D_CONV)),
            pl.BlockSpec(memory_space=pltpu.HBM),
            full((1, d_model)),
        ],
        out_specs=pl.BlockSpec((None, tile, d_model), lambda b, s: (b, s, 0)),
        scratch_shapes=[
            pltpu.VMEM((N_HEADS, HEAD_DIM, HEAD_DIM), jnp.float32),
            pltpu.VMEM((sub + SUBLANES, D_CONV), jnp.float32),
            pltpu.VMEM((sub, D_HGRN), jnp.float32),
            pltpu.VMEM((d_model, in_cols), jnp.bfloat16),
            pltpu.VMEM((D_HGRN + D_CONV, d_model), jnp.bfloat16),
            pltpu.VMEM((2, d_model, WEIGHT_SLAB), jnp.float32),
            pltpu.SemaphoreType.DMA((2,)),
        ],
    )
    return pl.pallas_call(
        _block_kernel,
        grid_spec=grid_spec,
        out_shape=jax.ShapeDtypeStruct(x.shape, x.dtype),
        compiler_params=pltpu.CompilerParams(
            dimension_semantics=("arbitrary", "arbitrary"),
            vmem_limit_bytes=VMEM_LIMIT_BYTES),
        name="hgrn2_shortconv_block",
    )(x, norm_gain, w_in, lb_logits, jnp.transpose(conv_w, (1, 0, 2)), hgrn_norm_gain,
      conv_norm_gain, w_out, final_norm_gain.reshape(1, d_model))
```

```python
import jax
import jax.numpy as jnp
from jax import lax
from jax.experimental import pallas as pl
from jax.experimental.pallas import tpu as pltpu

D_HGRN = 512
D_CONV = 512
HEAD_DIM = 128
N_HEADS = D_HGRN // HEAD_DIM
CHUNK = 64
CONV_WIDTH = 3
CONV_GROUP_DIM = 64
EPS = 1e-6
SEQ_TILE = 1024
SUB_TILE = 512
LANES = 128
SUBLANES = 8
VMEM_LIMIT_BYTES = 60000 * 1024
WEIGHT_SLAB = 512

_NT = (((1,), (1,)), ((), ()))
_TN = (((0,), (0,)), ((), ()))


def _silu(z):
    hz = 0.5 * z
    return hz + hz * jnp.tanh(hz)


def _slab(x, r0, lb, g_ref, win_ref, cw_ref, hg_ref, cg_ref, wout_ref, fg_ref,
          st_ref, cu_ref, o_s, loader=None):
    f32, bf16 = jnp.float32, jnp.bfloat16
    n_rows = x.shape[0]

    ms = jnp.mean(x * x, axis=-1, keepdims=True)
    h = (x * lax.rsqrt(ms + EPS) * g_ref[...]).astype(bf16)

    def proj(j, width):
        if loader is not None:
            loader.need_w_in(j)
        return jnp.dot(h, win_ref[:, j * width:(j + 1) * width], preferred_element_type=f32)

    q = proj(0, D_HGRN)
    f = (0.5 + 0.5 * lb) + (0.5 - 0.5 * lb) * jnp.tanh(0.5 * proj(1, D_HGRN))
    lf = jnp.log(f)
    k = 1.0 - f
    v = proj(2, D_HGRN).astype(bf16)

    row = lax.broadcasted_iota(jnp.int32, (CHUNK, CHUNK), 0)
    col = lax.broadcasted_iota(jnp.int32, (CHUNK, CHUNK), 1)
    causal = row >= col
    tri = causal.astype(bf16)
    n_chunks = n_rows // CHUNK
    rows = [slice(c * CHUNK, (c + 1) * CHUNK) for c in range(n_chunks)]
    heads = [slice(hd * HEAD_DIM, (hd + 1) * HEAD_DIM) for hd in range(N_HEADS)]

    lf_hi = lf.astype(bf16)
    lf_lo = (lf - lf_hi.astype(f32)).astype(bf16)
    tri2 = jnp.concatenate([tri, tri], axis=1)
    b = jnp.concatenate(
        [jnp.dot(tri2, jnp.concatenate([lf_hi[r], lf_lo[r]], axis=0),
                 preferred_element_type=f32) for r in rows], axis=0)
    eb = jnp.exp(b)
    q_dec = (q * eb).astype(bf16)
    k_inv = k * jnp.exp(-b)
    decay = [eb[r][CHUNK - 1:CHUNK, :] for r in rows]
    k_end = [(k_inv[r] * decay[c]).astype(bf16) for c, r in enumerate(rows)]
    k_inv = k_inv.astype(bf16)

    scores = [[lax.dot_general(q_dec[r, sl], k_inv[r, sl], _NT, preferred_element_type=f32)
               for sl in heads] for r in rows]
    upd_t = [[lax.dot_general(v[r, sl], k_end[c][:, sl], _TN, preferred_element_type=f32)
              for sl in heads] for c, r in enumerate(rows)]
    state_before = [[None] * N_HEADS for _ in rows]
    for hd, sl in enumerate(heads):
        state_t = st_ref[hd]
        for c in range(n_chunks):
            state_before[c][hd] = state_t.T.astype(bf16)
            state_t = state_t * decay[c][:, sl] + upd_t[c][hd]
        st_ref[hd] = state_t
    for c, r in enumerate(rows):
        for hd, sl in enumerate(heads):
            sc = jnp.where(causal, scores[c][hd], 0.0).astype(bf16)
            o_s[r0 + c * CHUNK:r0 + (c + 1) * CHUNK, sl] = (
                jnp.dot(sc, v[r, sl], preferred_element_type=f32)
                + jnp.dot(q_dec[r, sl], state_before[c][hd], preferred_element_type=f32))

    hg = hg_ref[...]
    o_parts = []
    for sl in heads:
        oh = o_s[r0:r0 + n_rows, sl]
        oms = jnp.mean(oh * oh, axis=-1, keepdims=True)
        o_parts.append(oh * lax.rsqrt(oms + EPS) * hg[:, sl])
    o_a = (jnp.concatenate(o_parts, axis=-1) * _silu(proj(3, D_HGRN))).astype(bf16)

    cw = [cw_ref[j] for j in range(CONV_WIDTH)]
    cu = proj(6, D_CONV) * proj(4, D_CONV)
    c0 = r0 + SUBLANES
    cu_ref[c0:c0 + n_rows, :] = cu
    conv = (cw[2] * cu
            + cw[1] * cu_ref[c0 - 1:c0 - 1 + n_rows, :]
            + cw[0] * cu_ref[c0 - 2:c0 - 2 + n_rows, :])
    y = proj(5, D_CONV) * conv
    cg = cg_ref[...]
    lane = lax.broadcasted_iota(jnp.int32, (1, LANES), 1)
    first_group = lane < CONV_GROUP_DIM
    y_parts = []
    for j in range(D_CONV // LANES):
        sl = slice(j * LANES, (j + 1) * LANES)
        yj = y[:, sl]
        y2 = yj * yj
        tot = jnp.sum(y2, axis=-1, keepdims=True)
        lo = jnp.sum(jnp.where(first_group, y2, 0.0), axis=-1, keepdims=True)
        yms = jnp.where(first_group, lo, tot - lo) * (1.0 / CONV_GROUP_DIM)
        y_parts.append(yj * lax.rsqrt(yms + EPS) * cg[:, sl])
    o_b = (jnp.concatenate(y_parts, axis=-1) * _silu(proj(7, D_CONV))).astype(bf16)

    if loader is not None:
        loader.need_all()
    mix = (jnp.dot(o_a, wout_ref[0:D_HGRN, :], preferred_element_type=f32)
           + jnp.dot(o_b, wout_ref[D_HGRN:D_HGRN + D_CONV, :], preferred_element_type=f32))
    res = x + mix
    rms = jnp.mean(res * res, axis=-1, keepdims=True)
    return res * lax.rsqrt(rms + EPS) * fg_ref[...]


W_IN_USE_ORDER = (0, 1, 2, 3, 6, 4, 5, 7)


class _WeightLoader:
    def __init__(self, win_hbm, wout_hbm, win_ref, wout_ref, stage, sems):
        assert win_ref.shape[1] == len(W_IN_USE_ORDER) * WEIGHT_SLAB
        cols = lambda j: pl.ds(j * WEIGHT_SLAB, WEIGHT_SLAB)
        self.slabs = [(win_hbm.at[0, :, cols(j)], win_ref.at[:, cols(j)]) for j in W_IN_USE_ORDER]
        self.slabs += [(wout_hbm.at[0, :, cols(j)], wout_ref.at[:, cols(j)])
                       for j in range(wout_ref.shape[1] // WEIGHT_SLAB)]
        self.stage, self.sems = stage, sems
        self.n_ready = 0
        for i in range(min(2, len(self.slabs))):
            self._copy(i).start()

    def _copy(self, i):
        return pltpu.make_async_copy(self.slabs[i][0], self.stage.at[i % 2], self.sems.at[i % 2])

    def _need(self, n):
        while self.n_ready < n:
            i = self.n_ready
            self._copy(i).wait()
            self.slabs[i][1][...] = self.stage[i % 2].astype(jnp.bfloat16)
            if i + 2 < len(self.slabs):
                self._copy(i + 2).start()
            self.n_ready += 1

    def need_w_in(self, j):
        self._need(W_IN_USE_ORDER.index(j) + 1)

    def need_all(self):
        self._need(len(self.slabs))


def _block_kernel(x_ref, g_ref, win_hbm, lbl_ref, cw_ref, hg_ref, cg_ref, wout_hbm, fg_ref,
                  out_ref, st_ref, cu_ref, o_s, win_ref, wout_ref, stage, sems):
    tile = x_ref.shape[0]
    sub = min(SUB_TILE, tile)

    @pl.when(pl.program_id(1) == 0)
    def _reset_carries():
        st_ref[...] = jnp.zeros_like(st_ref)
        cu_ref[0:SUBLANES, :] = jnp.zeros((SUBLANES, D_CONV), jnp.float32)

    def run_step(loader):
        lbl = lbl_ref[...]
        lmax = jnp.max(lbl, axis=0, keepdims=True)
        lexp = jnp.exp(lbl - lmax)
        lb = lexp[0:1, :] / jnp.sum(lexp, axis=0, keepdims=True)
        for r0 in range(0, tile, sub):
            out_ref[r0:r0 + sub, :] = _slab(
                x_ref[r0:r0 + sub, :], r0, lb, g_ref, win_ref, cw_ref, hg_ref, cg_ref, wout_ref,
                fg_ref, st_ref, cu_ref, o_s, loader if r0 == 0 else None)
        cu_ref[0:SUBLANES, :] = cu_ref[tile:tile + SUBLANES, :]

    first_step = (pl.program_id(0) == 0) & (pl.program_id(1) == 0)

    @pl.when(first_step)
    def _step_with_weight_load():
        run_step(_WeightLoader(win_hbm, wout_hbm, win_ref, wout_ref, stage, sems))

    @pl.when(jnp.logical_not(first_step))
    def _step():
        run_step(None)


@jax.jit
def kernel(x, norm_gain, w_in, lb_logits, conv_w, hgrn_norm_gain, conv_norm_gain, w_out,
           final_norm_gain):
    bsz, seq, d_model = x.shape
    depth = norm_gain.shape[0]
    assert depth == 1 and lb_logits.shape == (depth + 1, D_HGRN)
    assert w_in.shape == (depth, d_model, 4 * D_HGRN + 4 * D_CONV)
    assert w_out.shape == (depth, D_HGRN + D_CONV, d_model)
    tile = min(SEQ_TILE, seq)
    sub = min(SUB_TILE, tile)
    assert seq % tile == 0 and tile % sub == 0 and sub % CHUNK == 0
    in_cols = w_in.shape[2]
    assert in_cols % WEIGHT_SLAB == 0 and d_model % WEIGHT_SLAB == 0

    full = lambda shape: pl.BlockSpec(shape, lambda b, s: (0,) * len(shape))
    grid_spec = pltpu.PrefetchScalarGridSpec(
        num_scalar_prefetch=0,
        grid=(bsz, seq // tile),
        in_specs=[
            pl.BlockSpec((None, tile, d_model), lambda b, s: (b, s, 0)),
            full((1, d_model)),
            pl.BlockSpec(memory_space=pltpu.HBM),
            full((depth + 1, D_HGRN)),
            full((CONV_WIDTH, 1, D_CONV)),
            full((1, D_HGRN)),
            full((1, D_CONV)),
            pl.BlockSpec(memory_space=pltpu.HBM),
            full((1, d_model)),
        ],
        out_specs=pl.BlockSpec((None, tile, d_model), lambda b, s: (b, s, 0)),
        scratch_shapes=[
            pltpu.VMEM((N_HEADS, HEAD_DIM, HEAD_DIM), jnp.float32),
            pltpu.VMEM((tile + SUBLANES, D_CONV), jnp.float32),
            pltpu.VMEM((tile, D_HGRN), jnp.float32),
            pltpu.VMEM((d_model, in_cols), jnp.bfloat16),
            pltpu.VMEM((D_HGRN + D_CONV, d_model), jnp.bfloat16),
            pltpu.VMEM((2, d_model, WEIGHT_SLAB), jnp.float32),
            pltpu.SemaphoreType.DMA((2,)),
        ],
    )
    return pl.pallas_call(
        _block_kernel,
        grid_spec=grid_spec,
        out_shape=jax.ShapeDtypeStruct(x.shape, x.dtype),
        compiler_params=pltpu.CompilerParams(
            dimension_semantics=("arbitrary", "arbitrary"),
            vmem_limit_bytes=VMEM_LIMIT_BYTES),
        name="hgrn2_shortconv_block",
    )(x, norm_gain, w_in, lb_logits, jnp.transpose(conv_w, (1, 0, 2)), hgrn_norm_gain,
      conv_norm_gain, w_out, final_norm_gain.reshape(1, d_model))
```

```python
import jax
import jax.numpy as jnp
from jax import lax
from jax.experimental import pallas as pl
from jax.experimental.pallas import tpu as pltpu

D_HGRN = 512
D_CONV = 512
HEAD_DIM = 128
N_HEADS = D_HGRN // HEAD_DIM
CHUNK = 64
CONV_WIDTH = 3
CONV_GROUP_DIM = 64
EPS = 1e-6
SEQ_TILE = 1024
SUB_TILE = 512
LANES = 128
SUBLANES = 8
VMEM_LIMIT_BYTES = 60000 * 1024
WEIGHT_SLAB = 512

_NT = (((1,), (1,)), ((), ()))
_TN = (((0,), (0,)), ((), ()))


def _silu(z):
    hz = 0.5 * z
    return hz + hz * jnp.tanh(hz)


def _slab(x, r0, lb, g_ref, win_ref, cw_ref, hg_ref, cg_ref, wout_ref, fg_ref,
          st_ref, cu_ref, o_s, loader=None):
    f32, bf16 = jnp.float32, jnp.bfloat16
    n_rows = x.shape[0]

    ms = jnp.mean(x * x, axis=-1, keepdims=True)
    h = (x * lax.rsqrt(ms + EPS) * g_ref[...]).astype(bf16)

    def proj(j, width):
        if loader is not None:
            loader.need_w_in(j)
        return jnp.dot(h, win_ref[:, j * width:(j + 1) * width], preferred_element_type=f32)

    q = proj(0, D_HGRN)
    f = (0.5 + 0.5 * lb) + (0.5 - 0.5 * lb) * jnp.tanh(0.5 * proj(1, D_HGRN))
    lf = jnp.log(f)
    k = 1.0 - f
    v = proj(2, D_HGRN).astype(bf16)
    u = proj(4, D_CONV)

    row = lax.broadcasted_iota(jnp.int32, (CHUNK, CHUNK), 0)
    col = lax.broadcasted_iota(jnp.int32, (CHUNK, CHUNK), 1)
    causal = row >= col
    tri = causal.astype(bf16)
    n_chunks = n_rows // CHUNK
    rows = [slice(c * CHUNK, (c + 1) * CHUNK) for c in range(n_chunks)]
    heads = [slice(hd * HEAD_DIM, (hd + 1) * HEAD_DIM) for hd in range(N_HEADS)]

    lf_hi = lf.astype(bf16)
    lf_lo = (lf - lf_hi.astype(f32)).astype(bf16)
    tri2 = jnp.concatenate([tri, tri], axis=1)
    b = jnp.concatenate(
        [jnp.dot(tri2, jnp.concatenate([lf_hi[r], lf_lo[r]], axis=0),
                 preferred_element_type=f32) for r in rows], axis=0)
    gate_c = proj(6, D_CONV)
    eb = jnp.exp(b)
    q_dec = (q * eb).astype(bf16)
    k_inv = k * jnp.exp(-b)
    decay = [eb[r][CHUNK - 1:CHUNK, :] for r in rows]
    k_end = [(k_inv[r] * decay[c]).astype(bf16) for c, r in enumerate(rows)]
    k_inv = k_inv.astype(bf16)

    scores = [[lax.dot_general(q_dec[r, sl], k_inv[r, sl], _NT, preferred_element_type=f32)
               for sl in heads] for r in rows]
    upd_t = [[lax.dot_general(v[r, sl], k_end[c][:, sl], _TN, preferred_element_type=f32)
              for sl in heads] for c, r in enumerate(rows)]
    gate_b = proj(5, D_CONV)
    state_before = [[None] * N_HEADS for _ in rows]
    for hd, sl in enumerate(heads):
        state_t = st_ref[hd]
        for c in range(n_chunks):
            state_before[c][hd] = state_t.T.astype(bf16)
            state_t = state_t * decay[c][:, sl] + upd_t[c][hd]
        st_ref[hd] = state_t
    for c, r in enumerate(rows):
        for hd, sl in enumerate(heads):
            sc = jnp.where(causal, scores[c][hd], 0.0).astype(bf16)
            o_s[r0 + c * CHUNK:r0 + (c + 1) * CHUNK, sl] = (
                jnp.dot(sc, v[r, sl], preferred_element_type=f32)
                + jnp.dot(q_dec[r, sl], state_before[c][hd], preferred_element_type=f32))
    z_b = proj(7, D_CONV)
    z_a = proj(3, D_HGRN)

    hg = hg_ref[...]
    o_parts = []
    for sl in heads:
        oh = o_s[r0:r0 + n_rows, sl]
        oms = jnp.mean(oh * oh, axis=-1, keepdims=True)
        o_parts.append(oh * lax.rsqrt(oms + EPS) * hg[:, sl])
    o_a = (jnp.concatenate(o_parts, axis=-1) * _silu(z_a)).astype(bf16)

    cw = [cw_ref[j] for j in range(CONV_WIDTH)]
    cu = gate_c * u
    c0 = r0 + SUBLANES
    cu_ref[c0:c0 + n_rows, :] = cu
    conv = (cw[2] * cu
            + cw[1] * cu_ref[c0 - 1:c0 - 1 + n_rows, :]
            + cw[0] * cu_ref[c0 - 2:c0 - 2 + n_rows, :])
    y = gate_b * conv
    cg = cg_ref[...]
    lane = lax.broadcasted_iota(jnp.int32, (1, LANES), 1)
    first_group = lane < CONV_GROUP_DIM
    y_parts = []
    for j in range(D_CONV // LANES):
        sl = slice(j * LANES, (j + 1) * LANES)
        yj = y[:, sl]
        y2 = yj * yj
        tot = jnp.sum(y2, axis=-1, keepdims=True)
        lo = jnp.sum(jnp.where(first_group, y2, 0.0), axis=-1, keepdims=True)
        yms = jnp.where(first_group, lo, tot - lo) * (1.0 / CONV_GROUP_DIM)
        y_parts.append(yj * lax.rsqrt(yms + EPS) * cg[:, sl])
    o_b = (jnp.concatenate(y_parts, axis=-1) * _silu(z_b)).astype(bf16)

    if loader is not None:
        loader.need_all()
    mix = (jnp.dot(o_a, wout_ref[0:D_HGRN, :], preferred_element_type=f32)
           + jnp.dot(o_b, wout_ref[D_HGRN:D_HGRN + D_CONV, :], preferred_element_type=f32))
    res = x + mix
    rms = jnp.mean(res * res, axis=-1, keepdims=True)
    return res * lax.rsqrt(rms + EPS) * fg_ref[...]


W_IN_USE_ORDER = (0, 1, 2, 4, 6, 5, 7, 3)


class _WeightLoader:
    def __init__(self, win_hbm, wout_hbm, win_ref, wout_ref, stage, sems):
        assert win_ref.shape[1] == len(W_IN_USE_ORDER) * WEIGHT_SLAB
        cols = lambda j: pl.ds(j * WEIGHT_SLAB, WEIGHT_SLAB)
        self.slabs = [(win_hbm.at[0, :, cols(j)], win_ref.at[:, cols(j)]) for j in W_IN_USE_ORDER]
        self.slabs += [(wout_hbm.at[0, :, cols(j)], wout_ref.at[:, cols(j)])
                       for j in range(wout_ref.shape[1] // WEIGHT_SLAB)]
        self.stage, self.sems = stage, sems
        self.n_ready = 0
        for i in range(min(2, len(self.slabs))):
            self._copy(i).start()

    def _copy(self, i):
        return pltpu.make_async_copy(self.slabs[i][0], self.stage.at[i % 2], self.sems.at[i % 2])

    def _need(self, n):
        while self.n_ready < n:
            i = self.n_ready
            self._copy(i).wait()
            self.slabs[i][1][...] = self.stage[i % 2].astype(jnp.bfloat16)
            if i + 2 < len(self.slabs):
                self._copy(i + 2).start()
            self.n_ready += 1

    def need_w_in(self, j):
        self._need(W_IN_USE_ORDER.index(j) + 1)

    def need_all(self):
        self._need(len(self.slabs))


def _block_kernel(x_ref, g_ref, win_hbm, lbl_ref, cw_ref, hg_ref, cg_ref, wout_hbm, fg_ref,
                  out_ref, st_ref, cu_ref, o_s, win_ref, wout_ref, stage, sems):
    tile = x_ref.shape[0]
    sub = min(SUB_TILE, tile)

    @pl.when(pl.program_id(1) == 0)
    def _reset_carries():
        st_ref[...] = jnp.zeros_like(st_ref)
        cu_ref[0:SUBLANES, :] = jnp.zeros((SUBLANES, D_CONV), jnp.float32)

    def run_step(loader):
        lbl = lbl_ref[...]
        lmax = jnp.max(lbl, axis=0, keepdims=True)
        lexp = jnp.exp(lbl - lmax)
        lb = lexp[0:1, :] / jnp.sum(lexp, axis=0, keepdims=True)
        for r0 in range(0, tile, sub):
            out_ref[r0:r0 + sub, :] = _slab(
                x_ref[r0:r0 + sub, :], r0, lb, g_ref, win_ref, cw_ref, hg_ref, cg_ref, wout_ref,
                fg_ref, st_ref, cu_ref, o_s, loader if r0 == 0 else None)
        cu_ref[0:SUBLANES, :] = cu_ref[tile:tile + SUBLANES, :]

    first_step = (pl.program_id(0) == 0) & (pl.program_id(1) == 0)

    @pl.when(first_step)
    def _step_with_weight_load():
        run_step(_WeightLoader(win_hbm, wout_hbm, win_ref, wout_ref, stage, sems))

    @pl.when(jnp.logical_not(first_step))
    def _step():
        run_step(None)


@jax.jit
def kernel(x, norm_gain, w_in, lb_logits, conv_w, hgrn_norm_gain, conv_norm_gain, w_out,
           final_norm_gain):
    bsz, seq, d_model = x.shape
    depth = norm_gain.shape[0]
    assert depth == 1 and lb_logits.shape == (depth + 1, D_HGRN)
    assert w_in.shape == (depth, d_model, 4 * D_HGRN + 4 * D_CONV)
    assert w_out.shape == (depth, D_HGRN + D_CONV, d_model)
    tile = min(SEQ_TILE, seq)
    sub = min(SUB_TILE, tile)
    assert seq % tile == 0 and tile % sub == 0 and sub % CHUNK == 0
    in_cols = w_in.shape[2]
    assert in_cols % WEIGHT_SLAB == 0 and d_model % WEIGHT_SLAB == 0

    full = lambda shape: pl.BlockSpec(shape, lambda b, s: (0,) * len(shape))
    grid_spec = pltpu.PrefetchScalarGridSpec(
        num_scalar_prefetch=0,
        grid=(bsz, seq // tile),
        in_specs=[
            pl.BlockSpec((None, tile, d_model), lambda b, s: (b, s, 0)),
            full((1, d_model)),
            pl.BlockSpec(memory_space=pltpu.HBM),
            full((depth + 1, D_HGRN)),
            full((CONV_WIDTH, 1, D_CONV)),
            full((1, D_HGRN)),
            full((1, D_CONV)),
            pl.BlockSpec(memory_space=pltpu.HBM),
            full((1, d_model)),
        ],
        out_specs=pl.BlockSpec((None, tile, d_model), lambda b, s: (b, s, 0)),
        scratch_shapes=[
            pltpu.VMEM((N_HEADS, HEAD_DIM, HEAD_DIM), jnp.float32),
            pltpu.VMEM((tile + SUBLANES, D_CONV), jnp.float32),
            pltpu.VMEM((tile, D_HGRN), jnp.float32),
            pltpu.VMEM((d_model, in_cols), jnp.bfloat16),
            pltpu.VMEM((D_HGRN + D_CONV, d_model), jnp.bfloat16),
            pltpu.VMEM((2, d_model, WEIGHT_SLAB), jnp.float32),
            pltpu.SemaphoreType.DMA((2,)),
        ],
    )
    return pl.pallas_call(
        _block_kernel,
        grid_spec=grid_spec,
        out_shape=jax.ShapeDtypeStruct(x.shape, x.dtype),
        compiler_params=pltpu.CompilerParams(
            dimension_semantics=("arbitrary", "arbitrary"),
            vmem_limit_bytes=VMEM_LIMIT_BYTES),
        name="hgrn2_shortconv_block",
    )(x, norm_gain, w_in, lb_logits, jnp.transpose(conv_w, (1, 0, 2)), hgrn_norm_gain,
      conv_norm_gain, w_out, final_norm_gain.reshape(1, d_model))
```

```python
import jax
import jax.numpy as jnp
from jax import lax
from jax.experimental import pallas as pl
from jax.experimental.pallas import tpu as pltpu

D_HGRN = 512
D_CONV = 512
HEAD_DIM = 128
N_HEADS = D_HGRN // HEAD_DIM
CHUNK = 64
CONV_WIDTH = 3
CONV_GROUP_DIM = 64
EPS = 1e-6
SEQ_TILE = 1024
SUB_TILE = 1024
LANES = 128
SUBLANES = 8
VMEM_LIMIT_BYTES = 60000 * 1024
WEIGHT_SLAB = 512

_NT = (((1,), (1,)), ((), ()))
_TN = (((0,), (0,)), ((), ()))


def _silu(z):
    hz = 0.5 * z
    return hz + hz * jnp.tanh(hz)


def _slab(x, r0, lb, g_ref, win_ref, cw_ref, hg_ref, cg_ref, wout_ref, fg_ref,
          st_ref, cu_ref, o_s, loader=None):
    f32, bf16 = jnp.float32, jnp.bfloat16
    n_rows = x.shape[0]

    ms = jnp.mean(x * x, axis=-1, keepdims=True)
    h = (x * lax.rsqrt(ms + EPS) * g_ref[...]).astype(bf16)

    def proj(j, width):
        if loader is not None:
            loader.need_w_in(j)
        return jnp.dot(h, win_ref[:, j * width:(j + 1) * width], preferred_element_type=f32)

    q = proj(0, D_HGRN)
    f = (0.5 + 0.5 * lb) + (0.5 - 0.5 * lb) * jnp.tanh(0.5 * proj(1, D_HGRN))
    lf = jnp.log(f)
    k = 1.0 - f
    v = proj(2, D_HGRN).astype(bf16)
    u = proj(4, D_CONV)

    row = lax.broadcasted_iota(jnp.int32, (CHUNK, CHUNK), 0)
    col = lax.broadcasted_iota(jnp.int32, (CHUNK, CHUNK), 1)
    causal = row >= col
    tri = causal.astype(bf16)
    n_chunks = n_rows // CHUNK
    rows = [slice(c * CHUNK, (c + 1) * CHUNK) for c in range(n_chunks)]
    heads = [slice(hd * HEAD_DIM, (hd + 1) * HEAD_DIM) for hd in range(N_HEADS)]

    lf_hi = lf.astype(bf16)
    lf_lo = (lf - lf_hi.astype(f32)).astype(bf16)
    tri2 = jnp.concatenate([tri, tri], axis=1)
    b = jnp.concatenate(
        [jnp.dot(tri2, jnp.concatenate([lf_hi[r], lf_lo[r]], axis=0),
                 preferred_element_type=f32) for r in rows], axis=0)
    gate_c = proj(6, D_CONV)
    eb = jnp.exp(b)
    q_dec = (q * eb).astype(bf16)
    k_inv = k * jnp.exp(-b)
    decay = [eb[r][CHUNK - 1:CHUNK, :] for r in rows]
    k_end = [(k_inv[r] * decay[c]).astype(bf16) for c, r in enumerate(rows)]
    k_inv = k_inv.astype(bf16)

    scores = [[lax.dot_general(q_dec[r, sl], k_inv[r, sl], _NT, preferred_element_type=f32)
               for sl in heads] for r in rows]
    upd_t = [[lax.dot_general(v[r, sl], k_end[c][:, sl], _TN, preferred_element_type=f32)
              for sl in heads] for c, r in enumerate(rows)]
    gate_b = proj(5, D_CONV)
    state_before = [[None] * N_HEADS for _ in rows]
    for hd, sl in enumerate(heads):
        state_t = st_ref[hd]
        for c in range(n_chunks):
            state_before[c][hd] = state_t.T.astype(bf16)
            state_t = state_t * decay[c][:, sl] + upd_t[c][hd]
        st_ref[hd] = state_t
    for c, r in enumerate(rows):
        for hd, sl in enumerate(heads):
            sc = jnp.where(causal, scores[c][hd], 0.0).astype(bf16)
            o_s[r0 + c * CHUNK:r0 + (c + 1) * CHUNK, sl] = (
                jnp.dot(sc, v[r, sl], preferred_element_type=f32)
                + jnp.dot(q_dec[r, sl], state_before[c][hd], preferred_element_type=f32))
    z_b = proj(7, D_CONV)
    z_a = proj(3, D_HGRN)

    hg = hg_ref[...]
    o_parts = []
    for sl in heads:
        oh = o_s[r0:r0 + n_rows, sl]
        oms = jnp.mean(oh * oh, axis=-1, keepdims=True)
        o_parts.append(oh * lax.rsqrt(oms + EPS) * hg[:, sl])
    o_a = (jnp.concatenate(o_parts, axis=-1) * _silu(z_a)).astype(bf16)

    cw = [cw_ref[j] for j in range(CONV_WIDTH)]
    cu = gate_c * u
    c0 = r0 + SUBLANES
    cu_ref[c0:c0 + n_rows, :] = cu
    conv = (cw[2] * cu
            + cw[1] * cu_ref[c0 - 1:c0 - 1 + n_rows, :]
            + cw[0] * cu_ref[c0 - 2:c0 - 2 + n_rows, :])
    y = gate_b * conv
    cg = cg_ref[...]
    lane = lax.broadcasted_iota(jnp.int32, (1, LANES), 1)
    first_group = lane < CONV_GROUP_DIM
    y_parts = []
    for j in range(D_CONV // LANES):
        sl = slice(j * LANES, (j + 1) * LANES)
        yj = y[:, sl]
        y2 = yj * yj
        tot = jnp.sum(y2, axis=-1, keepdims=True)
        lo = jnp.sum(jnp.where(first_group, y2, 0.0), axis=-1, keepdims=True)
        yms = jnp.where(first_group, lo, tot - lo) * (1.0 / CONV_GROUP_DIM)
        y_parts.append(yj * lax.rsqrt(yms + EPS) * cg[:, sl])
    o_b = (jnp.concatenate(y_parts, axis=-1) * _silu(z_b)).astype(bf16)

    if loader is not None:
        loader.need_all()
    mix = (jnp.dot(o_a, wout_ref[0:D_HGRN, :], preferred_element_type=f32)
           + jnp.dot(o_b, wout_ref[D_HGRN:D_HGRN + D_CONV, :], preferred_element_type=f32))
    res = x + mix
    rms = jnp.mean(res * res, axis=-1, keepdims=True)
    return res * lax.rsqrt(rms + EPS) * fg_ref[...]


W_IN_USE_ORDER = (0, 1, 2, 4, 6, 5, 7, 3)


class _WeightLoader:
    def __init__(self, win_hbm, wout_hbm, win_ref, wout_ref, stage, sems):
        assert win_ref.shape[1] == len(W_IN_USE_ORDER) * WEIGHT_SLAB
        cols = lambda j: pl.ds(j * WEIGHT_SLAB, WEIGHT_SLAB)
        self.slabs = [(win_hbm.at[0, :, cols(j)], win_ref.at[:, cols(j)]) for j in W_IN_USE_ORDER]
        self.slabs += [(wout_hbm.at[0, :, cols(j)], wout_ref.at[:, cols(j)])
                       for j in range(wout_ref.shape[1] // WEIGHT_SLAB)]
        self.stage, self.sems = stage, sems
        self.n_ready = 0
        for i in range(min(2, len(self.slabs))):
            self._copy(i).start()

    def _copy(self, i):
        return pltpu.make_async_copy(self.slabs[i][0], self.stage.at[i % 2], self.sems.at[i % 2])

    def _need(self, n):
        while self.n_ready < n:
            i = self.n_ready
            self._copy(i).wait()
            self.slabs[i][1][...] = self.stage[i % 2].astype(jnp.bfloat16)
            if i + 2 < len(self.slabs):
                self._copy(i + 2).start()
            self.n_ready += 1

    def need_w_in(self, j):
        self._need(W_IN_USE_ORDER.index(j) + 1)

    def need_all(self):
        self._need(len(self.slabs))


def _block_kernel(x_ref, g_ref, win_hbm, lbl_ref, cw_ref, hg_ref, cg_ref, wout_hbm, fg_ref,
                  out_ref, st_ref, cu_ref, o_s, win_ref, wout_ref, stage, sems):
    tile = x_ref.shape[0]
    sub = min(SUB_TILE, tile)

    @pl.when(pl.program_id(1) == 0)
    def _reset_carries():
        st_ref[...] = jnp.zeros_like(st_ref)
        cu_ref[0:SUBLANES, :] = jnp.zeros((SUBLANES, D_CONV), jnp.float32)

    def run_step(loader):
        lbl = lbl_ref[...]
        lmax = jnp.max(lbl, axis=0, keepdims=True)
        lexp = jnp.exp(lbl - lmax)
        lb = lexp[0:1, :] / jnp.sum(lexp, axis=0, keepdims=True)
        for r0 in range(0, tile, sub):
            out_ref[r0:r0 + sub, :] = _slab(
                x_ref[r0:r0 + sub, :], r0, lb, g_ref, win_ref, cw_ref, hg_ref, cg_ref, wout_ref,
                fg_ref, st_ref, cu_ref, o_s, loader if r0 == 0 else None)
        cu_ref[0:SUBLANES, :] = cu_ref[tile:tile + SUBLANES, :]

    first_step = (pl.program_id(0) == 0) & (pl.program_id(1) == 0)

    @pl.when(first_step)
    def _step_with_weight_load():
        run_step(_WeightLoader(win_hbm, wout_hbm, win_ref, wout_ref, stage, sems))

    @pl.when(jnp.logical_not(first_step))
    def _step():
        run_step(None)


@jax.jit
def kernel(x, norm_gain, w_in, lb_logits, conv_w, hgrn_norm_gain, conv_norm_gain, w_out,
           final_norm_gain):
    bsz, seq, d_model = x.shape
    depth = norm_gain.shape[0]
    assert depth == 1 and lb_logits.shape == (depth + 1, D_HGRN)
    assert w_in.shape == (depth, d_model, 4 * D_HGRN + 4 * D_CONV)
    assert w_out.shape == (depth, D_HGRN + D_CONV, d_model)
    tile = min(SEQ_TILE, seq)
    sub = min(SUB_TILE, tile)
    assert seq % tile == 0 and tile % sub == 0 and sub % CHUNK == 0
    in_cols = w_in.shape[2]
    assert in_cols % WEIGHT_SLAB == 0 and d_model % WEIGHT_SLAB == 0

    full = lambda shape: pl.BlockSpec(shape, lambda b, s: (0,) * len(shape))
    grid_spec = pltpu.PrefetchScalarGridSpec(
        num_scalar_prefetch=0,
        grid=(bsz, seq // tile),
        in_specs=[
            pl.BlockSpec((None, tile, d_model), lambda b, s: (b, s, 0)),
            full((1, d_model)),
            pl.BlockSpec(memory_space=pltpu.HBM),
            full((depth + 1, D_HGRN)),
            full((CONV_WIDTH, 1, D_CONV)),
            full((1, D_HGRN)),
            full((1, D_CONV)),
            pl.BlockSpec(memory_space=pltpu.HBM),
            full((1, d_model)),
        ],
        out_specs=pl.BlockSpec((None, tile, d_model), lambda b, s: (b, s, 0)),
        scratch_shapes=[
            pltpu.VMEM((N_HEADS, HEAD_DIM, HEAD_DIM), jnp.float32),
            pltpu.VMEM((tile + SUBLANES, D_CONV), jnp.float32),
            pltpu.VMEM((tile, D_HGRN), jnp.float32),
            pltpu.VMEM((d_model, in_cols), jnp.bfloat16),
            pltpu.VMEM((D_HGRN + D_CONV, d_model), jnp.bfloat16),
            pltpu.VMEM((2, d_model, WEIGHT_SLAB), jnp.float32),
            pltpu.SemaphoreType.DMA((2,)),
        ],
    )
    return pl.pallas_call(
        _block_kernel,
        grid_spec=grid_spec,
        out_shape=jax.ShapeDtypeStruct(x.shape, x.dtype),
        compiler_params=pltpu.CompilerParams(
            dimension_semantics=("arbitrary", "arbitrary"),
            vmem_limit_bytes=VMEM_LIMIT_BYTES),
        name="hgrn2_shortconv_block",
    )(x, norm_gain, w_in, lb_logits, jnp.transpose(conv_w, (1, 0, 2)), hgrn_norm_gain,
      conv_norm_gain, w_out, final_norm_gain.reshape(1, d_model))
```

```python
import jax
import jax.numpy as jnp
from jax import lax
from jax.experimental import pallas as pl
from jax.experimental.pallas import tpu as pltpu

D_HGRN = 512
D_CONV = 512
HEAD_DIM = 128
N_HEADS = D_HGRN // HEAD_DIM
CHUNK = 64
CONV_WIDTH = 3
CONV_GROUP_DIM = 64
EPS = 1e-6
SEQ_TILE = 1024
SUB_TILE = 1024
LANES = 128
SUBLANES = 8
VMEM_LIMIT_BYTES = 60000 * 1024
WEIGHT_SLAB = 512

_NT = (((1,), (1,)), ((), ()))
_TN = (((0,), (0,)), ((), ()))


def _silu(z):
    hz = 0.5 * z
    return hz + hz * jnp.tanh(hz)


def _slab(x, r0, lb, g_ref, win_ref, cw_ref, hg_ref, cg_ref, wout_ref, fg_ref,
          st_ref, cu_ref, o_s, loader=None):
    f32, bf16 = jnp.float32, jnp.bfloat16
    n_rows = x.shape[0]

    ms = jnp.mean(x * x, axis=-1, keepdims=True)
    h = (x * lax.rsqrt(ms + EPS) * g_ref[...]).astype(bf16)

    def proj(j, width):
        if loader is not None:
            loader.need_w_in(j)
        return jnp.dot(h, win_ref[:, j * width:(j + 1) * width], preferred_element_type=f32)

    q = proj(0, D_HGRN)
    f = (0.5 + 0.5 * lb) + (0.5 - 0.5 * lb) * jnp.tanh(0.5 * proj(1, D_HGRN))
    lf = jnp.log(f)
    k = 1.0 - f
    v = proj(2, D_HGRN).astype(bf16)

    row = lax.broadcasted_iota(jnp.int32, (CHUNK, CHUNK), 0)
    col = lax.broadcasted_iota(jnp.int32, (CHUNK, CHUNK), 1)
    causal = row >= col
    tri = causal.astype(bf16)
    n_chunks = n_rows // CHUNK
    rows = [slice(c * CHUNK, (c + 1) * CHUNK) for c in range(n_chunks)]
    heads = [slice(hd * HEAD_DIM, (hd + 1) * HEAD_DIM) for hd in range(N_HEADS)]

    lf_hi = lf.astype(bf16)
    lf_lo = (lf - lf_hi.astype(f32)).astype(bf16)
    tri2 = jnp.concatenate([tri, tri], axis=1)
    b = jnp.concatenate(
        [jnp.dot(tri2, jnp.concatenate([lf_hi[r], lf_lo[r]], axis=0),
                 preferred_element_type=f32) for r in rows], axis=0)
    eb = jnp.exp(b)
    q_dec = (q * eb).astype(bf16)
    k_inv = k * jnp.exp(-b)
    decay = [eb[r][CHUNK - 1:CHUNK, :] for r in rows]
    k_end = [(k_inv[r] * decay[c]).astype(bf16) for c, r in enumerate(rows)]
    k_inv = k_inv.astype(bf16)

    scores = [[lax.dot_general(q_dec[r, sl], k_inv[r, sl], _NT, preferred_element_type=f32)
               for sl in heads] for r in rows]
    upd_t = [[lax.dot_general(v[r, sl], k_end[c][:, sl], _TN, preferred_element_type=f32)
              for sl in heads] for c, r in enumerate(rows)]
    state_before = [[None] * N_HEADS for _ in rows]
    for hd, sl in enumerate(heads):
        state_t = st_ref[hd]
        for c in range(n_chunks):
            state_before[c][hd] = state_t.T.astype(bf16)
            state_t = state_t * decay[c][:, sl] + upd_t[c][hd]
        st_ref[hd] = state_t
    for c, r in enumerate(rows):
        for hd, sl in enumerate(heads):
            sc = jnp.where(causal, scores[c][hd], 0.0).astype(bf16)
            o_s[r0 + c * CHUNK:r0 + (c + 1) * CHUNK, sl] = (
                jnp.dot(sc, v[r, sl], preferred_element_type=f32)
                + jnp.dot(q_dec[r, sl], state_before[c][hd], preferred_element_type=f32))
    z_a = proj(3, D_HGRN)

    hg = hg_ref[...]
    o_parts = []
    for sl in heads:
        oh = o_s[r0:r0 + n_rows, sl]
        oms = jnp.mean(oh * oh, axis=-1, keepdims=True)
        o_parts.append(oh * lax.rsqrt(oms + EPS) * hg[:, sl])
    o_a = (jnp.concatenate(o_parts, axis=-1) * _silu(z_a)).astype(bf16)

    cw = [cw_ref[j] for j in range(CONV_WIDTH)]
    cu = proj(6, D_CONV) * proj(4, D_CONV)
    c0 = r0 + SUBLANES
    cu_ref[c0:c0 + n_rows, :] = cu
    conv = (cw[2] * cu
            + cw[1] * cu_ref[c0 - 1:c0 - 1 + n_rows, :]
            + cw[0] * cu_ref[c0 - 2:c0 - 2 + n_rows, :])
    y = proj(5, D_CONV) * conv
    cg = cg_ref[...]
    lane = lax.broadcasted_iota(jnp.int32, (1, LANES), 1)
    first_group = lane < CONV_GROUP_DIM
    y_parts = []
    for j in range(D_CONV // LANES):
        sl = slice(j * LANES, (j + 1) * LANES)
        yj = y[:, sl]
        y2 = yj * yj
        tot = jnp.sum(y2, axis=-1, keepdims=True)
        lo = jnp.sum(jnp.where(first_group, y2, 0.0), axis=-1, keepdims=True)
        yms = jnp.where(first_group, lo, tot - lo) * (1.0 / CONV_GROUP_DIM)
        y_parts.append(yj * lax.rsqrt(yms + EPS) * cg[:, sl])
    o_b = (jnp.concatenate(y_parts, axis=-1) * _silu(proj(7, D_CONV))).astype(bf16)

    if loader is not None:
        loader.need_all()
    mix = (jnp.dot(o_a, wout_ref[0:D_HGRN, :], preferred_element_type=f32)
           + jnp.dot(o_b, wout_ref[D_HGRN:D_HGRN + D_CONV, :], preferred_element_type=f32))
    res = x + mix
    rms = jnp.mean(res * res, axis=-1, keepdims=True)
    return res * lax.rsqrt(rms + EPS) * fg_ref[...]


W_IN_USE_ORDER = (0, 1, 2, 3, 6, 4, 5, 7)


class _WeightLoader:
    def __init__(self, win_hbm, wout_hbm, win_ref, wout_ref, stage, sems):
        assert win_ref.shape[1] == len(W_IN_USE_ORDER) * WEIGHT_SLAB
        cols = lambda j: pl.ds(j * WEIGHT_SLAB, WEIGHT_SLAB)
        self.slabs = [(win_hbm.at[0, :, cols(j)], win_ref.at[:, cols(j)]) for j in W_IN_USE_ORDER]
        self.slabs += [(wout_hbm.at[0, :, cols(j)], wout_ref.at[:, cols(j)])
                       for j in range(wout_ref.shape[1] // WEIGHT_SLAB)]
        self.stage, self.sems = stage, sems
        self.n_ready = 0
        for i in range(min(2, len(self.slabs))):
            self._copy(i).start()

    def _copy(self, i):
        return pltpu.make_async_copy(self.slabs[i][0], self.stage.at[i % 2], self.sems.at[i % 2])

    def _need(self, n):
        while self.n_ready < n:
            i = self.n_ready
            self._copy(i).wait()
            self.slabs[i][1][...] = self.stage[i % 2].astype(jnp.bfloat16)
            if i + 2 < len(self.slabs):
                self._copy(i + 2).start()
            self.n_ready += 1

    def need_w_in(self, j):
        self._need(W_IN_USE_ORDER.index(j) + 1)

    def need_all(self):
        self._need(len(self.slabs))


def _block_kernel(x_ref, g_ref, win_hbm, lbl_ref, cw_ref, hg_ref, cg_ref, wout_hbm, fg_ref,
                  out_ref, st_ref, cu_ref, o_s, win_ref, wout_ref, stage, sems):
    tile = x_ref.shape[0]
    sub = min(SUB_TILE, tile)

    @pl.when(pl.program_id(1) == 0)
    def _reset_carries():
        st_ref[...] = jnp.zeros_like(st_ref)
        cu_ref[0:SUBLANES, :] = jnp.zeros((SUBLANES, D_CONV), jnp.float32)

    def run_step(loader):
        lbl = lbl_ref[...]
        lmax = jnp.max(lbl, axis=0, keepdims=True)
        lexp = jnp.exp(lbl - lmax)
        lb = lexp[0:1, :] / jnp.sum(lexp, axis=0, keepdims=True)
        for r0 in range(0, tile, sub):
            out_ref[r0:r0 + sub, :] = _slab(
                x_ref[r0:r0 + sub, :], r0, lb, g_ref, win_ref, cw_ref, hg_ref, cg_ref, wout_ref,
                fg_ref, st_ref, cu_ref, o_s, loader if r0 == 0 else None)
        cu_ref[0:SUBLANES, :] = cu_ref[tile:tile + SUBLANES, :]

    first_step = (pl.program_id(0) == 0) & (pl.program_id(1) == 0)

    @pl.when(first_step)
    def _step_with_weight_load():
        run_step(_WeightLoader(win_hbm, wout_hbm, win_ref, wout_ref, stage, sems))

    @pl.when(jnp.logical_not(first_step))
    def _step():
        run_step(None)


@jax.jit
def kernel(x, norm_gain, w_in, lb_logits, conv_w, hgrn_norm_gain, conv_norm_gain, w_out,
           final_norm_gain):
    bsz, seq, d_model = x.shape
    depth = norm_gain.shape[0]
    assert depth == 1 and lb_logits.shape == (depth + 1, D_HGRN)
    assert w_in.shape == (depth, d_model, 4 * D_HGRN + 4 * D_CONV)
    assert w_out.shape == (depth, D_HGRN + D_CONV, d_model)
    tile = min(SEQ_TILE, seq)
    sub = min(SUB_TILE, tile)
    assert seq % tile == 0 and tile % sub == 0 and sub % CHUNK == 0
    in_cols = w_in.shape[2]
    assert in_cols % WEIGHT_SLAB == 0 and d_model % WEIGHT_SLAB == 0

    full = lambda shape: pl.BlockSpec(shape, lambda b, s: (0,) * len(shape))
    grid_spec = pltpu.PrefetchScalarGridSpec(
        num_scalar_prefetch=0,
        grid=(bsz, seq // tile),
        in_specs=[
            pl.BlockSpec((None, tile, d_model), lambda b, s: (b, s, 0)),
            full((1, d_model)),
            pl.BlockSpec(memory_space=pltpu.HBM),
            full((depth + 1, D_HGRN)),
            full((CONV_WIDTH, 1, D_CONV)),
            full((1, D_HGRN)),
            full((1, D_CONV)),
            pl.BlockSpec(memory_space=pltpu.HBM),
            full((1, d_model)),
        ],
        out_specs=pl.BlockSpec((None, tile, d_model), lambda b, s: (b, s, 0)),
        scratch_shapes=[
            pltpu.VMEM((N_HEADS, HEAD_DIM, HEAD_DIM), jnp.float32),
            pltpu.VMEM((tile + SUBLANES, D_CONV), jnp.float32),
            pltpu.VMEM((tile, D_HGRN), jnp.float32),
            pltpu.VMEM((d_model, in_cols), jnp.bfloat16),
            pltpu.VMEM((D_HGRN + D_CONV, d_model), jnp.bfloat16),
            pltpu.VMEM((2, d_model, WEIGHT_SLAB), jnp.float32),
            pltpu.SemaphoreType.DMA((2,)),
        ],
    )
    return pl.pallas_call(
        _block_kernel,
        grid_spec=grid_spec,
        out_shape=jax.ShapeDtypeStruct(x.shape, x.dtype),
        compiler_params=pltpu.CompilerParams(
            dimension_semantics=("arbitrary", "arbitrary"),
            vmem_limit_bytes=VMEM_LIMIT_BYTES),
        name="hgrn2_shortconv_block",
    )(x, norm_gain, w_in, lb_logits, jnp.transpose(conv_w, (1, 0, 2)), hgrn_norm_gain,
      conv_norm_gain, w_out, final_norm_gain.reshape(1, d_model))
```

```python
import jax
import jax.numpy as jnp
from jax import lax
from jax.experimental import pallas as pl
from jax.experimental.pallas import tpu as pltpu

D_HGRN = 512
D_CONV = 512
HEAD_DIM = 128
N_HEADS = D_HGRN // HEAD_DIM
CHUNK = 64
CONV_WIDTH = 3
CONV_GROUP_DIM = 64
EPS = 1e-6
SEQ_TILE = 1024
SUB_TILE = 1024
LANES = 128
SUBLANES = 8
VMEM_LIMIT_BYTES = 60000 * 1024
WEIGHT_SLAB = 512

_NT = (((1,), (1,)), ((), ()))
_TN = (((0,), (0,)), ((), ()))


def _silu(z):
    hz = 0.5 * z
    return hz + hz * jnp.tanh(hz)


def _slab(x, r0, lb, g_ref, win_ref, cw_ref, hg_ref, cg_ref, wout_ref,
          st_ref, cu_ref, o_s, loader=None, after_first_projections=None,
          before_out_projection=None):
    f32, bf16 = jnp.float32, jnp.bfloat16
    n_rows = x.shape[0]

    ms = jnp.mean(x * x, axis=-1, keepdims=True)
    h = (x * lax.rsqrt(ms + EPS) * g_ref[...]).astype(bf16)

    def proj(j, width):
        if loader is not None:
            loader.need_w_in(j)
        return jnp.dot(h, win_ref[:, j * width:(j + 1) * width], preferred_element_type=f32)

    q = proj(0, D_HGRN)
    f = (0.5 + 0.5 * lb) + (0.5 - 0.5 * lb) * jnp.tanh(0.5 * proj(1, D_HGRN))
    lf = jnp.log(f)
    k = 1.0 - f
    v = proj(2, D_HGRN).astype(bf16)
    u = proj(4, D_CONV)
    if after_first_projections is not None:
        after_first_projections()

    row = lax.broadcasted_iota(jnp.int32, (CHUNK, CHUNK), 0)
    col = lax.broadcasted_iota(jnp.int32, (CHUNK, CHUNK), 1)
    causal = row >= col
    tri = causal.astype(bf16)
    n_chunks = n_rows // CHUNK
    rows = [slice(c * CHUNK, (c + 1) * CHUNK) for c in range(n_chunks)]
    heads = [slice(hd * HEAD_DIM, (hd + 1) * HEAD_DIM) for hd in range(N_HEADS)]

    lf_hi = lf.astype(bf16)
    lf_lo = (lf - lf_hi.astype(f32)).astype(bf16)
    tri2 = jnp.concatenate([tri, tri], axis=1)
    b = jnp.concatenate(
        [jnp.dot(tri2, jnp.concatenate([lf_hi[r], lf_lo[r]], axis=0),
                 preferred_element_type=f32) for r in rows], axis=0)
    gate_c = proj(6, D_CONV)
    eb = jnp.exp(b)
    q_dec = (q * eb).astype(bf16)
    k_inv = k * jnp.exp(-b)
    decay = [eb[r][CHUNK - 1:CHUNK, :] for r in rows]
    k_end = [(k_inv[r] * decay[c]).astype(bf16) for c, r in enumerate(rows)]
    k_inv = k_inv.astype(bf16)

    scores = [[lax.dot_general(q_dec[r, sl], k_inv[r, sl], _NT, preferred_element_type=f32)
               for sl in heads] for r in rows]
    upd_t = [[lax.dot_general(v[r, sl], k_end[c][:, sl], _TN, preferred_element_type=f32)
              for sl in heads] for c, r in enumerate(rows)]
    gate_b = proj(5, D_CONV)
    state_before = [[None] * N_HEADS for _ in rows]
    for hd, sl in enumerate(heads):
        state_t = st_ref[hd]
        for c in range(n_chunks):
            state_before[c][hd] = state_t.T.astype(bf16)
            state_t = state_t * decay[c][:, sl] + upd_t[c][hd]
        st_ref[hd] = state_t
    for c, r in enumerate(rows):
        for hd, sl in enumerate(heads):
            sc = jnp.where(causal, scores[c][hd], 0.0).astype(bf16)
            o_s[r0 + c * CHUNK:r0 + (c + 1) * CHUNK, sl] = (
                jnp.dot(sc, v[r, sl], preferred_element_type=f32)
                + jnp.dot(q_dec[r, sl], state_before[c][hd], preferred_element_type=f32))
    z_b = proj(7, D_CONV)
    z_a = proj(3, D_HGRN)

    hg = hg_ref[...]
    o_parts = []
    for sl in heads:
        oh = o_s[r0:r0 + n_rows, sl]
        oms = jnp.mean(oh * oh, axis=-1, keepdims=True)
        o_parts.append(oh * lax.rsqrt(oms + EPS) * hg[:, sl])
    o_a = (jnp.concatenate(o_parts, axis=-1) * _silu(z_a)).astype(bf16)

    cw = [cw_ref[j] for j in range(CONV_WIDTH)]
    cu = gate_c * u
    c0 = r0 + SUBLANES
    cu_ref[c0:c0 + n_rows, :] = cu
    conv = (cw[2] * cu
            + cw[1] * cu_ref[c0 - 1:c0 - 1 + n_rows, :]
            + cw[0] * cu_ref[c0 - 2:c0 - 2 + n_rows, :])
    y = gate_b * conv
    cg = cg_ref[...]
    lane = lax.broadcasted_iota(jnp.int32, (1, LANES), 1)
    first_group = lane < CONV_GROUP_DIM
    y_parts = []
    for j in range(D_CONV // LANES):
        sl = slice(j * LANES, (j + 1) * LANES)
        yj = y[:, sl]
        y2 = yj * yj
        tot = jnp.sum(y2, axis=-1, keepdims=True)
        lo = jnp.sum(jnp.where(first_group, y2, 0.0), axis=-1, keepdims=True)
        yms = jnp.where(first_group, lo, tot - lo) * (1.0 / CONV_GROUP_DIM)
        y_parts.append(yj * lax.rsqrt(yms + EPS) * cg[:, sl])
    o_b = (jnp.concatenate(y_parts, axis=-1) * _silu(z_b)).astype(bf16)

    if loader is not None:
        loader.need_all()
    if before_out_projection is not None:
        before_out_projection()
    mix = (jnp.dot(o_a, wout_ref[0:D_HGRN, :], preferred_element_type=f32)
           + jnp.dot(o_b, wout_ref[D_HGRN:D_HGRN + D_CONV, :], preferred_element_type=f32))
    return x + mix


W_IN_USE_ORDER = (0, 1, 2, 4, 6, 5, 7, 3)


class _WeightLoader:
    def __init__(self, win_hbm, wout_hbm, win_ref, wout_ref, stage, sems):
        assert win_ref.shape[1] == len(W_IN_USE_ORDER) * WEIGHT_SLAB
        cols = lambda j: pl.ds(j * WEIGHT_SLAB, WEIGHT_SLAB)
        self.slabs = [(win_hbm.at[0, :, cols(j)], win_ref.at[:, cols(j)]) for j in W_IN_USE_ORDER]
        self.slabs += [(wout_hbm.at[0, :, cols(j)], wout_ref.at[:, cols(j)])
                       for j in range(wout_ref.shape[1] // WEIGHT_SLAB)]
        self.stage, self.sems = stage, sems
        self.n_ready = 0
        for i in range(min(2, len(self.slabs))):
            self._copy(i).start()

    def _copy(self, i):
        return pltpu.make_async_copy(self.slabs[i][0], self.stage.at[i % 2], self.sems.at[i % 2])

    def _need(self, n):
        while self.n_ready < n:
            i = self.n_ready
            self._copy(i).wait()
            self.slabs[i][1][...] = self.stage[i % 2].astype(jnp.bfloat16)
            if i + 2 < len(self.slabs):
                self._copy(i + 2).start()
            self.n_ready += 1

    def need_w_in(self, j):
        self._need(W_IN_USE_ORDER.index(j) + 1)

    def need_all(self):
        self._need(len(self.slabs))


def _block_kernel(x_ref, g_ref, win_hbm, lbl_ref, cw_ref, hg_ref, cg_ref, wout_hbm, fg_ref,
                  out_hbm, st_ref, cu_ref, o_s, win_ref, wout_ref, stage, sems, res_ref, out_sem):
    tile = x_ref.shape[0]
    sub = min(SUB_TILE, tile)
    n_seq_tiles = pl.num_programs(1)
    step = pl.program_id(0) * n_seq_tiles + pl.program_id(1)

    def out_copy(flat_tile):
        b = flat_tile // n_seq_tiles
        s = flat_tile % n_seq_tiles
        return pltpu.make_async_copy(
            res_ref, out_hbm.at[b, pl.ds(pl.multiple_of(s * tile, tile), tile), :], out_sem)

    def final_norm_in_place():
        res = res_ref[...]
        rms = jnp.mean(res * res, axis=-1, keepdims=True)
        res_ref[...] = res * lax.rsqrt(rms + EPS) * fg_ref[...]

    @pl.when(pl.program_id(1) == 0)
    def _reset_carries():
        st_ref[...] = jnp.zeros_like(st_ref)
        cu_ref[0:SUBLANES, :] = jnp.zeros((SUBLANES, D_CONV), jnp.float32)

    def run_step(loader, finish_previous):
        if finish_previous:
            final_norm_in_place()
        lbl = lbl_ref[...]
        lmax = jnp.max(lbl, axis=0, keepdims=True)
        lexp = jnp.exp(lbl - lmax)
        lb = lexp[0:1, :] / jnp.sum(lexp, axis=0, keepdims=True)
        assert tile == sub
        start_write_back = (lambda: out_copy(step - 1).start()) if finish_previous else None
        wait_write_back = (lambda: out_copy(step - 1).wait()) if finish_previous else None
        res_ref[...] = _slab(x_ref[...], 0, lb, g_ref, win_ref, cw_ref, hg_ref, cg_ref, wout_ref,
                             st_ref, cu_ref, o_s, loader, start_write_back, wait_write_back)
        cu_ref[0:SUBLANES, :] = cu_ref[tile:tile + SUBLANES, :]

    @pl.when(step == 0)
    def _first_step():
        run_step(_WeightLoader(win_hbm, wout_hbm, win_ref, wout_ref, stage, sems), False)

    @pl.when(step > 0)
    def _step():
        run_step(None, True)

    @pl.when(step == pl.num_programs(0) * n_seq_tiles - 1)
    def _finish_last_tile():
        final_norm_in_place()
        out_copy(step).start()
        out_copy(step).wait()


@jax.jit
def kernel(x, norm_gain, w_in, lb_logits, conv_w, hgrn_norm_gain, conv_norm_gain, w_out,
           final_norm_gain):
    bsz, seq, d_model = x.shape
    depth = norm_gain.shape[0]
    assert depth == 1 and lb_logits.shape == (depth + 1, D_HGRN)
    assert w_in.shape == (depth, d_model, 4 * D_HGRN + 4 * D_CONV)
    assert w_out.shape == (depth, D_HGRN + D_CONV, d_model)
    tile = min(SEQ_TILE, seq)
    sub = min(SUB_TILE, tile)
    assert seq % tile == 0 and tile % sub == 0 and sub % CHUNK == 0
    in_cols = w_in.shape[2]
    assert in_cols % WEIGHT_SLAB == 0 and d_model % WEIGHT_SLAB == 0

    full = lambda shape: pl.BlockSpec(shape, lambda b, s: (0,) * len(shape))
    grid_spec = pltpu.PrefetchScalarGridSpec(
        num_scalar_prefetch=0,
        grid=(bsz, seq // tile),
        in_specs=[
            pl.BlockSpec((None, tile, d_model), lambda b, s: (b, s, 0)),
            full((1, d_model)),
            pl.BlockSpec(memory_space=pltpu.HBM),
            full((depth + 1, D_HGRN)),
            full((CONV_WIDTH, 1, D_CONV)),
            full((1, D_HGRN)),
            full((1, D_CONV)),
            pl.BlockSpec(memory_space=pltpu.HBM),
            full((1, d_model)),
        ],
        out_specs=pl.BlockSpec(memory_space=pltpu.HBM),
        scratch_shapes=[
            pltpu.VMEM((N_HEADS, HEAD_DIM, HEAD_DIM), jnp.float32),
            pltpu.VMEM((tile + SUBLANES, D_CONV), jnp.float32),
            pltpu.VMEM((tile, D_HGRN), jnp.float32),
            pltpu.VMEM((d_model, in_cols), jnp.bfloat16),
            pltpu.VMEM((D_HGRN + D_CONV, d_model), jnp.bfloat16),
            pltpu.VMEM((2, d_model, WEIGHT_SLAB), jnp.float32),
            pltpu.SemaphoreType.DMA((2,)),
            pltpu.VMEM((tile, d_model), jnp.float32),
            pltpu.SemaphoreType.DMA,
        ],
    )
    return pl.pallas_call(
        _block_kernel,
        grid_spec=grid_spec,
        out_shape=jax.ShapeDtypeStruct(x.shape, x.dtype),
        compiler_params=pltpu.CompilerParams(
            dimension_semantics=("arbitrary", "arbitrary"),
            vmem_limit_bytes=VMEM_LIMIT_BYTES),
        name="hgrn2_shortconv_block",
    )(x, norm_gain, w_in, lb_logits, jnp.transpose(conv_w, (1, 0, 2)), hgrn_norm_gain,
      conv_norm_gain, w_out, final_norm_gain.reshape(1, d_model))
```

```python
import jax
import jax.numpy as jnp
from jax import lax
from jax.experimental import pallas as pl
from jax.experimental.pallas import tpu as pltpu

D_HGRN = 512
D_CONV = 512
HEAD_DIM = 128
N_HEADS = D_HGRN // HEAD_DIM
CHUNK = 64
CONV_WIDTH = 3
CONV_GROUP_DIM = 64
EPS = 1e-6
SEQ_TILE = 1024
SUB_TILE = 1024
LANES = 128
SUBLANES = 8
VMEM_LIMIT_BYTES = 60000 * 1024
WEIGHT_SLAB = 512

_NT = (((1,), (1,)), ((), ()))
_TN = (((0,), (0,)), ((), ()))


def _silu(z):
    hz = 0.5 * z
    return hz + hz * jnp.tanh(hz)


def _slab(x, r0, lb, g_ref, win_ref, cw_ref, hg_ref, cg_ref, wout_ref,
          st_ref, cu_ref, o_s, loader=None, after_first_projections=None,
          before_out_projection=None):
    f32, bf16 = jnp.float32, jnp.bfloat16
    n_rows = x.shape[0]

    ms = jnp.mean(x * x, axis=-1, keepdims=True)
    h = (x * lax.rsqrt(ms + EPS) * g_ref[...]).astype(bf16)

    def proj(j, width):
        if loader is not None:
            loader.need_w_in(j)
        return jnp.dot(h, win_ref[:, j * width:(j + 1) * width], preferred_element_type=f32)

    q = proj(0, D_HGRN)
    if after_first_projections is not None:
        after_first_projections()
    f = (0.5 + 0.5 * lb) + (0.5 - 0.5 * lb) * jnp.tanh(0.5 * proj(1, D_HGRN))
    lf = jnp.log(f)
    k = 1.0 - f
    v = proj(2, D_HGRN).astype(bf16)
    u = proj(4, D_CONV)

    row = lax.broadcasted_iota(jnp.int32, (CHUNK, CHUNK), 0)
    col = lax.broadcasted_iota(jnp.int32, (CHUNK, CHUNK), 1)
    causal = row >= col
    tri = causal.astype(bf16)
    n_chunks = n_rows // CHUNK
    rows = [slice(c * CHUNK, (c + 1) * CHUNK) for c in range(n_chunks)]
    heads = [slice(hd * HEAD_DIM, (hd + 1) * HEAD_DIM) for hd in range(N_HEADS)]

    lf_hi = lf.astype(bf16)
    lf_lo = (lf - lf_hi.astype(f32)).astype(bf16)
    tri2 = jnp.concatenate([tri, tri], axis=1)
    b = jnp.concatenate(
        [jnp.dot(tri2, jnp.concatenate([lf_hi[r], lf_lo[r]], axis=0),
                 preferred_element_type=f32) for r in rows], axis=0)
    gate_c = proj(6, D_CONV)
    eb = jnp.exp(b)
    q_dec = (q * eb).astype(bf16)
    k_inv = k * jnp.exp(-b)
    decay = [eb[r][CHUNK - 1:CHUNK, :] for r in rows]
    k_end = [(k_inv[r] * decay[c]).astype(bf16) for c, r in enumerate(rows)]
    k_inv = k_inv.astype(bf16)

    scores = [[lax.dot_general(q_dec[r, sl], k_inv[r, sl], _NT, preferred_element_type=f32)
               for sl in heads] for r in rows]
    upd_t = [[lax.dot_general(v[r, sl], k_end[c][:, sl], _TN, preferred_element_type=f32)
              for sl in heads] for c, r in enumerate(rows)]
    gate_b = proj(5, D_CONV)
    state_before = [[None] * N_HEADS for _ in rows]
    for hd, sl in enumerate(heads):
        state_t = st_ref[hd]
        for c in range(n_chunks):
            state_before[c][hd] = state_t.T.astype(bf16)
            state_t = state_t * decay[c][:, sl] + upd_t[c][hd]
        st_ref[hd] = state_t
    for c, r in enumerate(rows):
        for hd, sl in enumerate(heads):
            sc = jnp.where(causal, scores[c][hd], 0.0).astype(bf16)
            o_s[r0 + c * CHUNK:r0 + (c + 1) * CHUNK, sl] = (
                jnp.dot(sc, v[r, sl], preferred_element_type=f32)
                + jnp.dot(q_dec[r, sl], state_before[c][hd], preferred_element_type=f32))
    z_b = proj(7, D_CONV)
    z_a = proj(3, D_HGRN)

    hg = hg_ref[...]
    o_parts = []
    for sl in heads:
        oh = o_s[r0:r0 + n_rows, sl]
        oms = jnp.mean(oh * oh, axis=-1, keepdims=True)
        o_parts.append(oh * lax.rsqrt(oms + EPS) * hg[:, sl])
    o_a = (jnp.concatenate(o_parts, axis=-1) * _silu(z_a)).astype(bf16)

    cw = [cw_ref[j] for j in range(CONV_WIDTH)]
    cu = gate_c * u
    c0 = r0 + SUBLANES
    cu_ref[c0:c0 + n_rows, :] = cu
    conv = (cw[2] * cu
            + cw[1] * cu_ref[c0 - 1:c0 - 1 + n_rows, :]
            + cw[0] * cu_ref[c0 - 2:c0 - 2 + n_rows, :])
    y = gate_b * conv
    cg = cg_ref[...]
    lane = lax.broadcasted_iota(jnp.int32, (1, LANES), 1)
    first_group = lane < CONV_GROUP_DIM
    y_parts = []
    for j in range(D_CONV // LANES):
        sl = slice(j * LANES, (j + 1) * LANES)
        yj = y[:, sl]
        y2 = yj * yj
        tot = jnp.sum(y2, axis=-1, keepdims=True)
        lo = jnp.sum(jnp.where(first_group, y2, 0.0), axis=-1, keepdims=True)
        yms = jnp.where(first_group, lo, tot - lo) * (1.0 / CONV_GROUP_DIM)
        y_parts.append(yj * lax.rsqrt(yms + EPS) * cg[:, sl])
    o_b = (jnp.concatenate(y_parts, axis=-1) * _silu(z_b)).astype(bf16)

    if loader is not None:
        loader.need_all()
    if before_out_projection is not None:
        before_out_projection()
    mix = (jnp.dot(o_a, wout_ref[0:D_HGRN, :], preferred_element_type=f32)
           + jnp.dot(o_b, wout_ref[D_HGRN:D_HGRN + D_CONV, :], preferred_element_type=f32))
    return x + mix


W_IN_USE_ORDER = (0, 1, 2, 4, 6, 5, 7, 3)


class _WeightLoader:
    def __init__(self, win_hbm, wout_hbm, win_ref, wout_ref, stage, sems):
        assert win_ref.shape[1] == len(W_IN_USE_ORDER) * WEIGHT_SLAB
        cols = lambda j: pl.ds(j * WEIGHT_SLAB, WEIGHT_SLAB)
        self.slabs = [(win_hbm.at[0, :, cols(j)], win_ref.at[:, cols(j)]) for j in W_IN_USE_ORDER]
        self.slabs += [(wout_hbm.at[0, :, cols(j)], wout_ref.at[:, cols(j)])
                       for j in range(wout_ref.shape[1] // WEIGHT_SLAB)]
        self.stage, self.sems = stage, sems
        self.n_ready = 0
        for i in range(min(2, len(self.slabs))):
            self._copy(i).start()

    def _copy(self, i):
        return pltpu.make_async_copy(self.slabs[i][0], self.stage.at[i % 2], self.sems.at[i % 2])

    def _need(self, n):
        while self.n_ready < n:
            i = self.n_ready
            self._copy(i).wait()
            self.slabs[i][1][...] = self.stage[i % 2].astype(jnp.bfloat16)
            if i + 2 < len(self.slabs):
                self._copy(i + 2).start()
            self.n_ready += 1

    def need_w_in(self, j):
        self._need(W_IN_USE_ORDER.index(j) + 1)

    def need_all(self):
        self._need(len(self.slabs))


def _block_kernel(x_ref, g_ref, win_hbm, lbl_ref, cw_ref, hg_ref, cg_ref, wout_hbm, fg_ref,
                  out_hbm, st_ref, cu_ref, o_s, win_ref, wout_ref, stage, sems, res_ref, out_sem):
    tile = x_ref.shape[0]
    sub = min(SUB_TILE, tile)
    n_seq_tiles = pl.num_programs(1)
    step = pl.program_id(0) * n_seq_tiles + pl.program_id(1)

    def out_copy(flat_tile):
        b = flat_tile // n_seq_tiles
        s = flat_tile % n_seq_tiles
        return pltpu.make_async_copy(
            res_ref, out_hbm.at[b, pl.ds(pl.multiple_of(s * tile, tile), tile), :], out_sem)

    def final_norm_in_place():
        res = res_ref[...]
        rms = jnp.mean(res * res, axis=-1, keepdims=True)
        res_ref[...] = res * lax.rsqrt(rms + EPS) * fg_ref[...]

    @pl.when(pl.program_id(1) == 0)
    def _reset_carries():
        st_ref[...] = jnp.zeros_like(st_ref)
        cu_ref[0:SUBLANES, :] = jnp.zeros((SUBLANES, D_CONV), jnp.float32)

    def run_step(loader, finish_previous):
        lbl = lbl_ref[...]
        lmax = jnp.max(lbl, axis=0, keepdims=True)
        lexp = jnp.exp(lbl - lmax)
        lb = lexp[0:1, :] / jnp.sum(lexp, axis=0, keepdims=True)
        assert tile == sub
        def finish_previous_tile():
            final_norm_in_place()
            out_copy(step - 1).start()

        start_write_back = finish_previous_tile if finish_previous else None
        wait_write_back = (lambda: out_copy(step - 1).wait()) if finish_previous else None
        res_ref[...] = _slab(x_ref[...], 0, lb, g_ref, win_ref, cw_ref, hg_ref, cg_ref, wout_ref,
                             st_ref, cu_ref, o_s, loader, start_write_back, wait_write_back)
        cu_ref[0:SUBLANES, :] = cu_ref[tile:tile + SUBLANES, :]

    @pl.when(step == 0)
    def _first_step():
        run_step(_WeightLoader(win_hbm, wout_hbm, win_ref, wout_ref, stage, sems), False)

    @pl.when(step > 0)
    def _step():
        run_step(None, True)

    @pl.when(step == pl.num_programs(0) * n_seq_tiles - 1)
    def _finish_last_tile():
        final_norm_in_place()
        out_copy(step).start()
        out_copy(step).wait()


@jax.jit
def kernel(x, norm_gain, w_in, lb_logits, conv_w, hgrn_norm_gain, conv_norm_gain, w_out,
           final_norm_gain):
    bsz, seq, d_model = x.shape
    depth = norm_gain.shape[0]
    assert depth == 1 and lb_logits.shape == (depth + 1, D_HGRN)
    assert w_in.shape == (depth, d_model, 4 * D_HGRN + 4 * D_CONV)
    assert w_out.shape == (depth, D_HGRN + D_CONV, d_model)
    tile = min(SEQ_TILE, seq)
    sub = min(SUB_TILE, tile)
    assert seq % tile == 0 and tile % sub == 0 and sub % CHUNK == 0
    in_cols = w_in.shape[2]
    assert in_cols % WEIGHT_SLAB == 0 and d_model % WEIGHT_SLAB == 0

    full = lambda shape: pl.BlockSpec(shape, lambda b, s: (0,) * len(shape))
    grid_spec = pltpu.PrefetchScalarGridSpec(
        num_scalar_prefetch=0,
        grid=(bsz, seq // tile),
        in_specs=[
            pl.BlockSpec((None, tile, d_model), lambda b, s: (b, s, 0)),
            full((1, d_model)),
            pl.BlockSpec(memory_space=pltpu.HBM),
            full((depth + 1, D_HGRN)),
            full((CONV_WIDTH, 1, D_CONV)),
            full((1, D_HGRN)),
            full((1, D_CONV)),
            pl.BlockSpec(memory_space=pltpu.HBM),
            full((1, d_model)),
        ],
        out_specs=pl.BlockSpec(memory_space=pltpu.HBM),
        scratch_shapes=[
            pltpu.VMEM((N_HEADS, HEAD_DIM, HEAD_DIM), jnp.float32),
            pltpu.VMEM((tile + SUBLANES, D_CONV), jnp.float32),
            pltpu.VMEM((tile, D_HGRN), jnp.float32),
            pltpu.VMEM((d_model, in_cols), jnp.bfloat16),
            pltpu.VMEM((D_HGRN + D_CONV, d_model), jnp.bfloat16),
            pltpu.VMEM((2, d_model, WEIGHT_SLAB), jnp.float32),
            pltpu.SemaphoreType.DMA((2,)),
            pltpu.VMEM((tile, d_model), jnp.float32),
            pltpu.SemaphoreType.DMA,
        ],
    )
    return pl.pallas_call(
        _block_kernel,
        grid_spec=grid_spec,
        out_shape=jax.ShapeDtypeStruct(x.shape, x.dtype),
        compiler_params=pltpu.CompilerParams(
            dimension_semantics=("arbitrary", "arbitrary"),
            vmem_limit_bytes=VMEM_LIMIT_BYTES),
        name="hgrn2_shortconv_block",
    )(x, norm_gain, w_in, lb_logits, jnp.transpose(conv_w, (1, 0, 2)), hgrn_norm_gain,
      conv_norm_gain, w_out, final_norm_gain.reshape(1, d_model))
```

```python
import functools

import jax
import jax.numpy as jnp
from jax import lax
from jax.experimental import pallas as pl
from jax.experimental.pallas import tpu as pltpu

D_HGRN = 512
D_CONV = 512
HEAD_DIM = 128
N_HEADS = D_HGRN // HEAD_DIM
CHUNK = 64
CONV_WIDTH = 3
CONV_GROUP_DIM = 64
EPS = 1e-6
SEQ_TILE = 1024
SUB_TILE = 1024
LANES = 128
SUBLANES = 8
VMEM_LIMIT_BYTES = 60000 * 1024
WEIGHT_SLAB = 512

_NT = (((1,), (1,)), ((), ()))
_TN = (((0,), (0,)), ((), ()))


def _silu(z):
    hz = 0.5 * z
    return hz + hz * jnp.tanh(hz)


def _slab(x, r0, lb, g_ref, win_ref, cw_ref, hg_ref, cg_ref, wout_ref,
          st_ref, cu_ref, o_s, loader=None, after_first_projection=None):
    f32, bf16 = jnp.float32, jnp.bfloat16
    n_rows = x.shape[0]

    ms = jnp.mean(x * x, axis=-1, keepdims=True)
    h = (x * lax.rsqrt(ms + EPS) * g_ref[...]).astype(bf16)

    def proj(j, width):
        if loader is not None:
            loader.need_w_in(j)
        return jnp.dot(h, win_ref[:, j * width:(j + 1) * width], preferred_element_type=f32)

    q = proj(0, D_HGRN)
    if after_first_projection is not None:
        after_first_projection()
    f = (0.5 + 0.5 * lb) + (0.5 - 0.5 * lb) * jnp.tanh(0.5 * proj(1, D_HGRN))
    lf = jnp.log(f)
    k = 1.0 - f
    v = proj(2, D_HGRN).astype(bf16)
    u = proj(4, D_CONV)

    row = lax.broadcasted_iota(jnp.int32, (CHUNK, CHUNK), 0)
    col = lax.broadcasted_iota(jnp.int32, (CHUNK, CHUNK), 1)
    causal = row >= col
    tri = causal.astype(bf16)
    n_chunks = n_rows // CHUNK
    rows = [slice(c * CHUNK, (c + 1) * CHUNK) for c in range(n_chunks)]
    heads = [slice(hd * HEAD_DIM, (hd + 1) * HEAD_DIM) for hd in range(N_HEADS)]

    lf_hi = lf.astype(bf16)
    lf_lo = (lf - lf_hi.astype(f32)).astype(bf16)
    tri2 = jnp.concatenate([tri, tri], axis=1)
    b = jnp.concatenate(
        [jnp.dot(tri2, jnp.concatenate([lf_hi[r], lf_lo[r]], axis=0),
                 preferred_element_type=f32) for r in rows], axis=0)
    gate_c = proj(6, D_CONV)
    eb = jnp.exp(b)
    q_dec = (q * eb).astype(bf16)
    k_inv = k * jnp.exp(-b)
    decay = [eb[r][CHUNK - 1:CHUNK, :] for r in rows]
    k_end = [(k_inv[r] * decay[c]).astype(bf16) for c, r in enumerate(rows)]
    k_inv = k_inv.astype(bf16)

    scores = [[lax.dot_general(q_dec[r, sl], k_inv[r, sl], _NT, preferred_element_type=f32)
               for sl in heads] for r in rows]
    upd_t = [[lax.dot_general(v[r, sl], k_end[c][:, sl], _TN, preferred_element_type=f32)
              for sl in heads] for c, r in enumerate(rows)]
    gate_b = proj(5, D_CONV)
    state_before = [[None] * N_HEADS for _ in rows]
    for hd, sl in enumerate(heads):
        state_t = st_ref[hd]
        for c in range(n_chunks):
            state_before[c][hd] = state_t.T.astype(bf16)
            state_t = state_t * decay[c][:, sl] + upd_t[c][hd]
        st_ref[hd] = state_t
    for c, r in enumerate(rows):
        for hd, sl in enumerate(heads):
            sc = jnp.where(causal, scores[c][hd], 0.0).astype(bf16)
            o_s[r0 + c * CHUNK:r0 + (c + 1) * CHUNK, sl] = (
                jnp.dot(sc, v[r, sl], preferred_element_type=f32)
                + jnp.dot(q_dec[r, sl], state_before[c][hd], preferred_element_type=f32))
    z_b = proj(7, D_CONV)
    z_a = proj(3, D_HGRN)

    hg = hg_ref[...]
    o_parts = []
    for sl in heads:
        oh = o_s[r0:r0 + n_rows, sl]
        oms = jnp.mean(oh * oh, axis=-1, keepdims=True)
        o_parts.append(oh * lax.rsqrt(oms + EPS) * hg[:, sl])
    o_a = (jnp.concatenate(o_parts, axis=-1) * _silu(z_a)).astype(bf16)

    cw = [cw_ref[j] for j in range(CONV_WIDTH)]
    cu = gate_c * u
    c0 = r0 + SUBLANES
    cu_ref[c0:c0 + n_rows, :] = cu
    conv = (cw[2] * cu
            + cw[1] * cu_ref[c0 - 1:c0 - 1 + n_rows, :]
            + cw[0] * cu_ref[c0 - 2:c0 - 2 + n_rows, :])
    y = gate_b * conv
    cg = cg_ref[...]
    lane = lax.broadcasted_iota(jnp.int32, (1, LANES), 1)
    first_group = lane < CONV_GROUP_DIM
    y_parts = []
    for j in range(D_CONV // LANES):
        sl = slice(j * LANES, (j + 1) * LANES)
        yj = y[:, sl]
        y2 = yj * yj
        tot = jnp.sum(y2, axis=-1, keepdims=True)
        lo = jnp.sum(jnp.where(first_group, y2, 0.0), axis=-1, keepdims=True)
        yms = jnp.where(first_group, lo, tot - lo) * (1.0 / CONV_GROUP_DIM)
        y_parts.append(yj * lax.rsqrt(yms + EPS) * cg[:, sl])
    o_b = (jnp.concatenate(y_parts, axis=-1) * _silu(z_b)).astype(bf16)

    if loader is not None:
        loader.need_all()
    mix = (jnp.dot(o_a, wout_ref[0:D_HGRN, :], preferred_element_type=f32)
           + jnp.dot(o_b, wout_ref[D_HGRN:D_HGRN + D_CONV, :], preferred_element_type=f32))
    return x + mix


W_IN_USE_ORDER = (0, 1, 2, 4, 6, 5, 7, 3)


class _WeightLoader:
    def __init__(self, win_hbm, wout_hbm, win_ref, wout_ref, stage, sems):
        assert win_ref.shape[1] == len(W_IN_USE_ORDER) * WEIGHT_SLAB
        cols = lambda j: pl.ds(j * WEIGHT_SLAB, WEIGHT_SLAB)
        self.slabs = [(win_hbm.at[0, :, cols(j)], win_ref.at[:, cols(j)]) for j in W_IN_USE_ORDER]
        self.slabs += [(wout_hbm.at[0, :, cols(j)], wout_ref.at[:, cols(j)])
                       for j in range(wout_ref.shape[1] // WEIGHT_SLAB)]
        self.stage, self.sems = stage, sems
        self.n_ready = 0
        for i in range(min(2, len(self.slabs))):
            self._copy(i).start()

    def _copy(self, i):
        return pltpu.make_async_copy(self.slabs[i][0], self.stage.at[i % 2], self.sems.at[i % 2])

    def _need(self, n):
        while self.n_ready < n:
            i = self.n_ready
            self._copy(i).wait()
            self.slabs[i][1][...] = self.stage[i % 2].astype(jnp.bfloat16)
            if i + 2 < len(self.slabs):
                self._copy(i + 2).start()
            self.n_ready += 1

    def need_w_in(self, j):
        self._need(W_IN_USE_ORDER.index(j) + 1)

    def need_all(self):
        self._need(len(self.slabs))


def _block_kernel(n_seq_tiles, x_ref, g_ref, win_hbm, lbl_ref, cw_ref, hg_ref, cg_ref, wout_hbm,
                  fg_ref, out_ref, st_ref, cu_ref, o_s, win_ref, wout_ref, stage, sems, res_ref):
    tile = x_ref.shape[0]
    step = pl.program_id(0)
    n_tiles = pl.num_programs(0) - 1

    def write_previous_tile():
        res = res_ref[...]
        rms = jnp.mean(res * res, axis=-1, keepdims=True)
        out_ref[...] = res * lax.rsqrt(rms + EPS) * fg_ref[...]

    @pl.when((step % n_seq_tiles == 0) & (step < n_tiles))
    def _reset_carries():
        st_ref[...] = jnp.zeros_like(st_ref)
        cu_ref[0:SUBLANES, :] = jnp.zeros((SUBLANES, D_CONV), jnp.float32)

    def run_step(loader, finish_previous):
        lbl = lbl_ref[...]
        lmax = jnp.max(lbl, axis=0, keepdims=True)
        lexp = jnp.exp(lbl - lmax)
        lb = lexp[0:1, :] / jnp.sum(lexp, axis=0, keepdims=True)
        res_ref[...] = _slab(x_ref[...], 0, lb, g_ref, win_ref, cw_ref, hg_ref, cg_ref, wout_ref,
                             st_ref, cu_ref, o_s, loader,
                             write_previous_tile if finish_previous else None)
        cu_ref[0:SUBLANES, :] = cu_ref[tile:tile + SUBLANES, :]

    @pl.when(step == 0)
    def _first_step():
        run_step(_WeightLoader(win_hbm, wout_hbm, win_ref, wout_ref, stage, sems), False)

    @pl.when((step > 0) & (step < n_tiles))
    def _step():
        run_step(None, True)

    @pl.when(step == n_tiles)
    def _finish_last_tile():
        write_previous_tile()


@jax.jit
def kernel(x, norm_gain, w_in, lb_logits, conv_w, hgrn_norm_gain, conv_norm_gain, w_out,
           final_norm_gain):
    bsz, seq, d_model = x.shape
    depth = norm_gain.shape[0]
    assert depth == 1 and lb_logits.shape == (depth + 1, D_HGRN)
    assert w_in.shape == (depth, d_model, 4 * D_HGRN + 4 * D_CONV)
    assert w_out.shape == (depth, D_HGRN + D_CONV, d_model)
    tile = min(SEQ_TILE, seq)
    sub = min(SUB_TILE, tile)
    assert seq % tile == 0 and tile == sub and sub % CHUNK == 0
    n_seq_tiles = seq // tile
    n_tiles = bsz * n_seq_tiles
    in_cols = w_in.shape[2]
    assert in_cols % WEIGHT_SLAB == 0 and d_model % WEIGHT_SLAB == 0

    full = lambda shape: pl.BlockSpec(shape, lambda i: (0,) * len(shape))

    def x_tile(i):
        t = jnp.minimum(i, n_tiles - 1)
        return t // n_seq_tiles, t % n_seq_tiles, 0

    def out_tile(i):
        t = jnp.maximum(i - 1, 0)
        return t // n_seq_tiles, t % n_seq_tiles, 0

    grid_spec = pltpu.PrefetchScalarGridSpec(
        num_scalar_prefetch=0,
        grid=(n_tiles + 1,),
        in_specs=[
            pl.BlockSpec((None, tile, d_model), x_tile),
            full((1, d_model)),
            pl.BlockSpec(memory_space=pltpu.HBM),
            full((depth + 1, D_HGRN)),
            full((CONV_WIDTH, 1, D_CONV)),
            full((1, D_HGRN)),
            full((1, D_CONV)),
            pl.BlockSpec(memory_space=pltpu.HBM),
            full((1, d_model)),
        ],
        out_specs=pl.BlockSpec((None, tile, d_model), out_tile),
        scratch_shapes=[
            pltpu.VMEM((N_HEADS, HEAD_DIM, HEAD_DIM), jnp.float32),
            pltpu.VMEM((tile + SUBLANES, D_CONV), jnp.float32),
            pltpu.VMEM((tile, D_HGRN), jnp.float32),
            pltpu.VMEM((d_model, in_cols), jnp.bfloat16),
            pltpu.VMEM((D_HGRN + D_CONV, d_model), jnp.bfloat16),
            pltpu.VMEM((2, d_model, WEIGHT_SLAB), jnp.float32),
            pltpu.SemaphoreType.DMA((2,)),
            pltpu.VMEM((tile, d_model), jnp.float32),
        ],
    )
    return pl.pallas_call(
        functools.partial(_block_kernel, n_seq_tiles),
        grid_spec=grid_spec,
        out_shape=jax.ShapeDtypeStruct(x.shape, x.dtype),
        compiler_params=pltpu.CompilerParams(
            dimension_semantics=("arbitrary",),
            vmem_limit_bytes=VMEM_LIMIT_BYTES),
        name="hgrn2_shortconv_block",
    )(x, norm_gain, w_in, lb_logits, jnp.transpose(conv_w, (1, 0, 2)), hgrn_norm_gain,
      conv_norm_gain, w_out, final_norm_gain.reshape(1, d_model))
```

```python
import jax
import jax.numpy as jnp
from jax import lax
from jax.experimental import pallas as pl
from jax.experimental.pallas import tpu as pltpu

D_HGRN = 512
D_CONV = 512
HEAD_DIM = 128
N_HEADS = D_HGRN // HEAD_DIM
CHUNK = 64
CONV_WIDTH = 3
CONV_GROUP_DIM = 64
EPS = 1e-6
SEQ_TILE = 1024
SUB_TILE = 1024
LANES = 128
SUBLANES = 8
VMEM_LIMIT_BYTES = 60000 * 1024
WEIGHT_SLAB = 512

_NT = (((1,), (1,)), ((), ()))
_TN = (((0,), (0,)), ((), ()))


def _silu(z):
    hz = 0.5 * z
    return hz + hz * jnp.tanh(hz)


def _slab(x, r0, lb, g_ref, win_ref, cw_ref, hg_ref, cg_ref, wout_ref, fg_ref,
          st_ref, cu_ref, o_s, loader=None):
    f32, bf16 = jnp.float32, jnp.bfloat16
    n_rows = x.shape[0]

    ms = jnp.mean(x * x, axis=-1, keepdims=True)
    h = (x * lax.rsqrt(ms + EPS) * g_ref[...]).astype(bf16)

    def proj(j, width):
        if loader is not None:
            loader.need_w_in(j)
        return jnp.dot(h, win_ref[:, j * width:(j + 1) * width], preferred_element_type=f32)

    q = proj(0, D_HGRN)
    f = (0.5 + 0.5 * lb) + (0.5 - 0.5 * lb) * jnp.tanh(0.5 * proj(1, D_HGRN))
    lf = jnp.log(f)
    k = 1.0 - f
    v = proj(2, D_HGRN).astype(bf16)
    u = proj(4, D_CONV)

    row = lax.broadcasted_iota(jnp.int32, (CHUNK, CHUNK), 0)
    col = lax.broadcasted_iota(jnp.int32, (CHUNK, CHUNK), 1)
    causal = row >= col
    tri = causal.astype(bf16)
    n_chunks = n_rows // CHUNK
    rows = [slice(c * CHUNK, (c + 1) * CHUNK) for c in range(n_chunks)]
    heads = [slice(hd * HEAD_DIM, (hd + 1) * HEAD_DIM) for hd in range(N_HEADS)]

    lf_hi = lf.astype(bf16)
    lf_lo = (lf - lf_hi.astype(f32)).astype(bf16)
    tri2 = jnp.concatenate([tri, tri], axis=1)
    b = jnp.concatenate(
        [jnp.dot(tri2, jnp.concatenate([lf_hi[r], lf_lo[r]], axis=0),
                 preferred_element_type=f32) for r in rows], axis=0)
    gate_c = proj(6, D_CONV)
    eb = jnp.exp(b)
    q_dec = (q * eb).astype(bf16)
    k_inv = k * jnp.exp(-b)
    decay = [eb[r][CHUNK - 1:CHUNK, :] for r in rows]
    k_end = [(k_inv[r] * decay[c]).astype(bf16) for c, r in enumerate(rows)]
    k_inv = k_inv.astype(bf16)

    scores = [[lax.dot_general(q_dec[r, sl], k_inv[r, sl], _NT, preferred_element_type=f32)
               for sl in heads] for r in rows]
    upd_t = [[lax.dot_general(v[r, sl], k_end[c][:, sl], _TN, preferred_element_type=f32)
              for sl in heads] for c, r in enumerate(rows)]
    gate_b = proj(5, D_CONV)
    state_before = [[None] * N_HEADS for _ in rows]
    for hd, sl in enumerate(heads):
        state_t = st_ref[hd]
        for c in range(n_chunks):
            state_before[c][hd] = state_t.T.astype(bf16)
            state_t = state_t * decay[c][:, sl] + upd_t[c][hd]
        st_ref[hd] = state_t
    for c, r in enumerate(rows):
        for hd, sl in enumerate(heads):
            sc = jnp.where(causal, scores[c][hd], 0.0).astype(bf16)
            o_s[r0 + c * CHUNK:r0 + (c + 1) * CHUNK, sl] = (
                jnp.dot(sc, v[r, sl], preferred_element_type=f32)
                + jnp.dot(q_dec[r, sl], state_before[c][hd], preferred_element_type=f32))
    z_b = proj(7, D_CONV)
    z_a = proj(3, D_HGRN)

    hg = hg_ref[...]
    o_parts = []
    for sl in heads:
        oh = o_s[r0:r0 + n_rows, sl]
        oms = jnp.mean(oh * oh, axis=-1, keepdims=True)
        o_parts.append(oh * lax.rsqrt(oms + EPS) * hg[:, sl])
    o_a = (jnp.concatenate(o_parts, axis=-1) * _silu(z_a)).astype(bf16)

    cw = [cw_ref[j] for j in range(CONV_WIDTH)]
    cu = gate_c * u
    cu_ref[SUBLANES:2 * SUBLANES, :] = cu[0:SUBLANES]
    head1 = cu_ref[SUBLANES - 1:2 * SUBLANES - 1, :]
    head2 = cu_ref[SUBLANES - 2:2 * SUBLANES - 2, :]
    cu_ref[0:SUBLANES, :] = cu[n_rows - SUBLANES:n_rows]
    prev1 = jnp.concatenate([head1, pltpu.roll(cu, 1, axis=0)[SUBLANES:]], axis=0)
    prev2 = jnp.concatenate([head2, pltpu.roll(cu, 2, axis=0)[SUBLANES:]], axis=0)
    conv = cw[2] * cu + cw[1] * prev1 + cw[0] * prev2
    y = gate_b * conv
    cg = cg_ref[...]
    lane = lax.broadcasted_iota(jnp.int32, (1, LANES), 1)
    first_group = lane < CONV_GROUP_DIM
    y_parts = []
    for j in range(D_CONV // LANES):
        sl = slice(j * LANES, (j + 1) * LANES)
        yj = y[:, sl]
        y2 = yj * yj
        tot = jnp.sum(y2, axis=-1, keepdims=True)
        lo = jnp.sum(jnp.where(first_group, y2, 0.0), axis=-1, keepdims=True)
        yms = jnp.where(first_group, lo, tot - lo) * (1.0 / CONV_GROUP_DIM)
        y_parts.append(yj * lax.rsqrt(yms + EPS) * cg[:, sl])
    o_b = (jnp.concatenate(y_parts, axis=-1) * _silu(z_b)).astype(bf16)

    if loader is not None:
        loader.need_all()
    mix = (jnp.dot(o_a, wout_ref[0:D_HGRN, :], preferred_element_type=f32)
           + jnp.dot(o_b, wout_ref[D_HGRN:D_HGRN + D_CONV, :], preferred_element_type=f32))
    res = x + mix
    rms = jnp.mean(res * res, axis=-1, keepdims=True)
    return res * lax.rsqrt(rms + EPS) * fg_ref[...]


W_IN_USE_ORDER = (0, 1, 2, 4, 6, 5, 7, 3)


class _WeightLoader:
    def __init__(self, win_hbm, wout_hbm, win_ref, wout_ref, stage, sems):
        assert win_ref.shape[1] == len(W_IN_USE_ORDER) * WEIGHT_SLAB
        cols = lambda j: pl.ds(j * WEIGHT_SLAB, WEIGHT_SLAB)
        self.slabs = [(win_hbm.at[0, :, cols(j)], win_ref.at[:, cols(j)]) for j in W_IN_USE_ORDER]
        self.slabs += [(wout_hbm.at[0, :, cols(j)], wout_ref.at[:, cols(j)])
                       for j in range(wout_ref.shape[1] // WEIGHT_SLAB)]
        self.stage, self.sems = stage, sems
        self.n_ready = 0
        for i in range(min(2, len(self.slabs))):
            self._copy(i).start()

    def _copy(self, i):
        return pltpu.make_async_copy(self.slabs[i][0], self.stage.at[i % 2], self.sems.at[i % 2])

    def _need(self, n):
        while self.n_ready < n:
            i = self.n_ready
            self._copy(i).wait()
            self.slabs[i][1][...] = self.stage[i % 2].astype(jnp.bfloat16)
            if i + 2 < len(self.slabs):
                self._copy(i + 2).start()
            self.n_ready += 1

    def need_w_in(self, j):
        self._need(W_IN_USE_ORDER.index(j) + 1)

    def need_all(self):
        self._need(len(self.slabs))


def _block_kernel(x_ref, g_ref, win_hbm, lbl_ref, cw_ref, hg_ref, cg_ref, wout_hbm, fg_ref,
                  out_ref, st_ref, cu_ref, o_s, win_ref, wout_ref, stage, sems):
    tile = x_ref.shape[0]
    sub = min(SUB_TILE, tile)

    @pl.when(pl.program_id(1) == 0)
    def _reset_carries():
        st_ref[...] = jnp.zeros_like(st_ref)
        cu_ref[0:SUBLANES, :] = jnp.zeros((SUBLANES, D_CONV), jnp.float32)

    def run_step(loader):
        lbl = lbl_ref[...]
        lmax = jnp.max(lbl, axis=0, keepdims=True)
        lexp = jnp.exp(lbl - lmax)
        lb = lexp[0:1, :] / jnp.sum(lexp, axis=0, keepdims=True)
        for r0 in range(0, tile, sub):
            out_ref[r0:r0 + sub, :] = _slab(
                x_ref[r0:r0 + sub, :], r0, lb, g_ref, win_ref, cw_ref, hg_ref, cg_ref, wout_ref,
                fg_ref, st_ref, cu_ref, o_s, loader if r0 == 0 else None)

    first_step = (pl.program_id(0) == 0) & (pl.program_id(1) == 0)

    @pl.when(first_step)
    def _step_with_weight_load():
        run_step(_WeightLoader(win_hbm, wout_hbm, win_ref, wout_ref, stage, sems))

    @pl.when(jnp.logical_not(first_step))
    def _step():
        run_step(None)


@jax.jit
def kernel(x, norm_gain, w_in, lb_logits, conv_w, hgrn_norm_gain, conv_norm_gain, w_out,
           final_norm_gain):
    bsz, seq, d_model = x.shape
    depth = norm_gain.shape[0]
    assert depth == 1 and lb_logits.shape == (depth + 1, D_HGRN)
    assert w_in.shape == (depth, d_model, 4 * D_HGRN + 4 * D_CONV)
    assert w_out.shape == (depth, D_HGRN + D_CONV, d_model)
    tile = min(SEQ_TILE, seq)
    sub = min(SUB_TILE, tile)
    assert seq % tile == 0 and tile % sub == 0 and sub % CHUNK == 0
    in_cols = w_in.shape[2]
    assert in_cols % WEIGHT_SLAB == 0 and d_model % WEIGHT_SLAB == 0

    full = lambda shape: pl.BlockSpec(shape, lambda b, s: (0,) * len(shape))
    grid_spec = pltpu.PrefetchScalarGridSpec(
        num_scalar_prefetch=0,
        grid=(bsz, seq // tile),
        in_specs=[
            pl.BlockSpec((None, tile, d_model), lambda b, s: (b, s, 0)),
            full((1, d_model)),
            pl.BlockSpec(memory_space=pltpu.HBM),
            full((depth + 1, D_HGRN)),
            full((CONV_WIDTH, 1, D_CONV)),
            full((1, D_HGRN)),
            full((1, D_CONV)),
            pl.BlockSpec(memory_space=pltpu.HBM),
            full((1, d_model)),
        ],
        out_specs=pl.BlockSpec((None, tile, d_model), lambda b, s: (b, s, 0)),
        scratch_shapes=[
            pltpu.VMEM((N_HEADS, HEAD_DIM, HEAD_DIM), jnp.float32),
            pltpu.VMEM((2 * SUBLANES, D_CONV), jnp.float32),
            pltpu.VMEM((tile, D_HGRN), jnp.float32),
            pltpu.VMEM((d_model, in_cols), jnp.bfloat16),
            pltpu.VMEM((D_HGRN + D_CONV, d_model), jnp.bfloat16),
            pltpu.VMEM((2, d_model, WEIGHT_SLAB), jnp.float32),
            pltpu.SemaphoreType.DMA((2,)),
        ],
    )
    return pl.pallas_call(
        _block_kernel,
        grid_spec=grid_spec,
        out_shape=jax.ShapeDtypeStruct(x.shape, x.dtype),
        compiler_params=pltpu.CompilerParams(
            dimension_semantics=("arbitrary", "arbitrary"),
            vmem_limit_bytes=VMEM_LIMIT_BYTES),
        name="hgrn2_shortconv_block",
    )(x, norm_gain, w_in, lb_logits, jnp.transpose(conv_w, (1, 0, 2)), hgrn_norm_gain,
      conv_norm_gain, w_out, final_norm_gain.reshape(1, d_model))
```

```python
import functools

import jax
import jax.numpy as jnp
from jax import lax
from jax.experimental import pallas as pl
from jax.experimental.pallas import tpu as pltpu

D_HGRN = 512
D_CONV = 512
HEAD_DIM = 128
N_HEADS = D_HGRN // HEAD_DIM
CHUNK = 64
CONV_WIDTH = 3
CONV_GROUP_DIM = 64
EPS = 1e-6
SEQ_TILE = 1024
SUB_TILE = 1024
LANES = 128
SUBLANES = 8
VMEM_LIMIT_BYTES = 60000 * 1024
WEIGHT_SLAB = 512

_NT = (((1,), (1,)), ((), ()))
_TN = (((0,), (0,)), ((), ()))


def _silu(z):
    hz = 0.5 * z
    return hz + hz * jnp.tanh(hz)


def _slab(x, r0, lb, g_ref, win_ref, cw_ref, hg_ref, cg_ref, wout_ref,
          st_ref, cu_ref, o_s, loader=None, after_first_projection=None):
    f32, bf16 = jnp.float32, jnp.bfloat16
    n_rows = x.shape[0]

    ms = jnp.mean(x * x, axis=-1, keepdims=True)
    h = (x * lax.rsqrt(ms + EPS) * g_ref[...]).astype(bf16)

    def proj(j, width):
        if loader is not None:
            loader.need_w_in(j)
        return jnp.dot(h, win_ref[:, j * width:(j + 1) * width], preferred_element_type=f32)

    q = proj(0, D_HGRN)
    f_logit = proj(1, D_HGRN)
    if after_first_projection is not None:
        zero = after_first_projection()
        f_logit = f_logit + jnp.concatenate([zero] * (D_HGRN // LANES), axis=1)
    f = (0.5 + 0.5 * lb) + (0.5 - 0.5 * lb) * jnp.tanh(0.5 * f_logit)
    lf = jnp.log(f)
    k = 1.0 - f
    v = proj(2, D_HGRN).astype(bf16)
    u = proj(4, D_CONV)

    row = lax.broadcasted_iota(jnp.int32, (CHUNK, CHUNK), 0)
    col = lax.broadcasted_iota(jnp.int32, (CHUNK, CHUNK), 1)
    causal = row >= col
    tri = causal.astype(bf16)
    n_chunks = n_rows // CHUNK
    rows = [slice(c * CHUNK, (c + 1) * CHUNK) for c in range(n_chunks)]
    heads = [slice(hd * HEAD_DIM, (hd + 1) * HEAD_DIM) for hd in range(N_HEADS)]

    lf_hi = lf.astype(bf16)
    lf_lo = (lf - lf_hi.astype(f32)).astype(bf16)
    tri2 = jnp.concatenate([tri, tri], axis=1)
    b = jnp.concatenate(
        [jnp.dot(tri2, jnp.concatenate([lf_hi[r], lf_lo[r]], axis=0),
                 preferred_element_type=f32) for r in rows], axis=0)
    gate_c = proj(6, D_CONV)
    eb = jnp.exp(b)
    q_dec = (q * eb).astype(bf16)
    k_inv = k * jnp.exp(-b)
    decay = [eb[r][CHUNK - 1:CHUNK, :] for r in rows]
    k_end = [(k_inv[r] * decay[c]).astype(bf16) for c, r in enumerate(rows)]
    k_inv = k_inv.astype(bf16)

    scores = [[lax.dot_general(q_dec[r, sl], k_inv[r, sl], _NT, preferred_element_type=f32)
               for sl in heads] for r in rows]
    upd_t = [[lax.dot_general(v[r, sl], k_end[c][:, sl], _TN, preferred_element_type=f32)
              for sl in heads] for c, r in enumerate(rows)]
    gate_b = proj(5, D_CONV)
    state_before = [[None] * N_HEADS for _ in rows]
    for hd, sl in enumerate(heads):
        state_t = st_ref[hd]
        for c in range(n_chunks):
            state_before[c][hd] = state_t.T.astype(bf16)
            state_t = state_t * decay[c][:, sl] + upd_t[c][hd]
        st_ref[hd] = state_t
    for c, r in enumerate(rows):
        for hd, sl in enumerate(heads):
            sc = jnp.where(causal, scores[c][hd], 0.0).astype(bf16)
            o_s[r0 + c * CHUNK:r0 + (c + 1) * CHUNK, sl] = (
                jnp.dot(sc, v[r, sl], preferred_element_type=f32)
                + jnp.dot(q_dec[r, sl], state_before[c][hd], preferred_element_type=f32))
    z_b = proj(7, D_CONV)
    z_a = proj(3, D_HGRN)

    hg = hg_ref[...]
    o_parts = []
    for sl in heads:
        oh = o_s[r0:r0 + n_rows, sl]
        oms = jnp.mean(oh * oh, axis=-1, keepdims=True)
        o_parts.append(oh * lax.rsqrt(oms + EPS) * hg[:, sl])
    o_a = (jnp.concatenate(o_parts, axis=-1) * _silu(z_a)).astype(bf16)

    cw = [cw_ref[j] for j in range(CONV_WIDTH)]
    cu = gate_c * u
    c0 = r0 + SUBLANES
    cu_ref[c0:c0 + n_rows, :] = cu
    conv = (cw[2] * cu
            + cw[1] * cu_ref[c0 - 1:c0 - 1 + n_rows, :]
            + cw[0] * cu_ref[c0 - 2:c0 - 2 + n_rows, :])
    y = gate_b * conv
    cg = cg_ref[...]
    lane = lax.broadcasted_iota(jnp.int32, (1, LANES), 1)
    first_group = lane < CONV_GROUP_DIM
    y_parts = []
    for j in range(D_CONV // LANES):
        sl = slice(j * LANES, (j + 1) * LANES)
        yj = y[:, sl]
        y2 = yj * yj
        tot = jnp.sum(y2, axis=-1, keepdims=True)
        lo = jnp.sum(jnp.where(first_group, y2, 0.0), axis=-1, keepdims=True)
        yms = jnp.where(first_group, lo, tot - lo) * (1.0 / CONV_GROUP_DIM)
        y_parts.append(yj * lax.rsqrt(yms + EPS) * cg[:, sl])
    o_b = (jnp.concatenate(y_parts, axis=-1) * _silu(z_b)).astype(bf16)

    if loader is not None:
        loader.need_all()
    mix = (jnp.dot(o_a, wout_ref[0:D_HGRN, :], preferred_element_type=f32)
           + jnp.dot(o_b, wout_ref[D_HGRN:D_HGRN + D_CONV, :], preferred_element_type=f32))
    return x + mix


W_IN_USE_ORDER = (0, 1, 2, 4, 6, 5, 7, 3)


class _WeightLoader:
    def __init__(self, win_hbm, wout_hbm, win_ref, wout_ref, stage, sems):
        assert win_ref.shape[1] == len(W_IN_USE_ORDER) * WEIGHT_SLAB
        cols = lambda j: pl.ds(j * WEIGHT_SLAB, WEIGHT_SLAB)
        self.slabs = [(win_hbm.at[0, :, cols(j)], win_ref.at[:, cols(j)]) for j in W_IN_USE_ORDER]
        self.slabs += [(wout_hbm.at[0, :, cols(j)], wout_ref.at[:, cols(j)])
                       for j in range(wout_ref.shape[1] // WEIGHT_SLAB)]
        self.stage, self.sems = stage, sems
        self.n_ready = 0
        for i in range(min(2, len(self.slabs))):
            self._copy(i).start()

    def _copy(self, i):
        return pltpu.make_async_copy(self.slabs[i][0], self.stage.at[i % 2], self.sems.at[i % 2])

    def _need(self, n):
        while self.n_ready < n:
            i = self.n_ready
            self._copy(i).wait()
            self.slabs[i][1][...] = self.stage[i % 2].astype(jnp.bfloat16)
            if i + 2 < len(self.slabs):
                self._copy(i + 2).start()
            self.n_ready += 1

    def need_w_in(self, j):
        self._need(W_IN_USE_ORDER.index(j) + 1)

    def need_all(self):
        self._need(len(self.slabs))


def _block_kernel(n_seq_tiles, x_ref, g_ref, win_hbm, lbl_ref, cw_ref, hg_ref, cg_ref, wout_hbm,
                  fg_ref, out_ref, st_ref, cu_ref, o_s, win_ref, wout_ref, stage, sems, res_ref):
    tile = x_ref.shape[0]
    step = pl.program_id(0)
    n_tiles = pl.num_programs(0) - 1

    def write_previous_tile():
        res = res_ref[...]
        rms = jnp.mean(res * res, axis=-1, keepdims=True)
        out = res * lax.rsqrt(rms + EPS) * fg_ref[...]
        out_ref[...] = out
        folded = out[:, 0:LANES]
        for j in range(1, out.shape[1] // LANES):
            folded = folded + out[:, j * LANES:(j + 1) * LANES]
        bits = lax.bitcast_convert_type(folded, jnp.int32)
        bits = lax.shift_right_logical(lax.shift_right_logical(bits, 16), 16)
        return bits.astype(jnp.float32)

    @pl.when((step % n_seq_tiles == 0) & (step < n_tiles))
    def _reset_carries():
        st_ref[...] = jnp.zeros_like(st_ref)
        cu_ref[0:SUBLANES, :] = jnp.zeros((SUBLANES, D_CONV), jnp.float32)

    def run_step(loader, finish_previous):
        lbl = lbl_ref[...]
        lmax = jnp.max(lbl, axis=0, keepdims=True)
        lexp = jnp.exp(lbl - lmax)
        lb = lexp[0:1, :] / jnp.sum(lexp, axis=0, keepdims=True)
        res_ref[...] = _slab(x_ref[...], 0, lb, g_ref, win_ref, cw_ref, hg_ref, cg_ref, wout_ref,
                             st_ref, cu_ref, o_s, loader,
                             write_previous_tile if finish_previous else None)
        cu_ref[0:SUBLANES, :] = cu_ref[tile:tile + SUBLANES, :]

    @pl.when(step == 0)
    def _first_step():
        run_step(_WeightLoader(win_hbm, wout_hbm, win_ref, wout_ref, stage, sems), False)

    @pl.when((step > 0) & (step < n_tiles))
    def _step():
        run_step(None, True)

    @pl.when(step == n_tiles)
    def _finish_last_tile():
        write_previous_tile()


@jax.jit
def kernel(x, norm_gain, w_in, lb_logits, conv_w, hgrn_norm_gain, conv_norm_gain, w_out,
           final_norm_gain):
    bsz, seq, d_model = x.shape
    depth = norm_gain.shape[0]
    assert depth == 1 and lb_logits.shape == (depth + 1, D_HGRN)
    assert w_in.shape == (depth, d_model, 4 * D_HGRN + 4 * D_CONV)
    assert w_out.shape == (depth, D_HGRN + D_CONV, d_model)
    tile = min(SEQ_TILE, seq)
    sub = min(SUB_TILE, tile)
    assert seq % tile == 0 and tile == sub and sub % CHUNK == 0
    n_seq_tiles = seq // tile
    n_tiles = bsz * n_seq_tiles
    in_cols = w_in.shape[2]
    assert in_cols % WEIGHT_SLAB == 0 and d_model % WEIGHT_SLAB == 0

    full = lambda shape: pl.BlockSpec(shape, lambda i: (0,) * len(shape))

    def x_tile(i):
        t = jnp.minimum(i, n_tiles - 1)
        return t // n_seq_tiles, t % n_seq_tiles, 0

    def out_tile(i):
        t = jnp.maximum(i - 1, 0)
        return t // n_seq_tiles, t % n_seq_tiles, 0

    grid_spec = pltpu.PrefetchScalarGridSpec(
        num_scalar_prefetch=0,
        grid=(n_tiles + 1,),
        in_specs=[
            pl.BlockSpec((None, tile, d_model), x_tile),
            full((1, d_model)),
            pl.BlockSpec(memory_space=pltpu.HBM),
            full((depth + 1, D_HGRN)),
            full((CONV_WIDTH, 1, D_CONV)),
            full((1, D_HGRN)),
            full((1, D_CONV)),
            pl.BlockSpec(memory_space=pltpu.HBM),
            full((1, d_model)),
        ],
        out_specs=pl.BlockSpec((None, tile, d_model), out_tile),
        scratch_shapes=[
            pltpu.VMEM((N_HEADS, HEAD_DIM, HEAD_DIM), jnp.float32),
            pltpu.VMEM((tile + SUBLANES, D_CONV), jnp.float32),
            pltpu.VMEM((tile, D_HGRN), jnp.float32),
            pltpu.VMEM((d_model, in_cols), jnp.bfloat16),
            pltpu.VMEM((D_HGRN + D_CONV, d_model), jnp.bfloat16),
            pltpu.VMEM((2, d_model, WEIGHT_SLAB), jnp.float32),
            pltpu.SemaphoreType.DMA((2,)),
            pltpu.VMEM((tile, d_model), jnp.float32),
        ],
    )
    return pl.pallas_call(
        functools.partial(_block_kernel, n_seq_tiles),
        grid_spec=grid_spec,
        out_shape=jax.ShapeDtypeStruct(x.shape, x.dtype),
        compiler_params=pltpu.CompilerParams(
            dimension_semantics=("arbitrary",),
            vmem_limit_bytes=VMEM_LIMIT_BYTES),
        name="hgrn2_shortconv_block",
    )(x, norm_gain, w_in, lb_logits, jnp.transpose(conv_w, (1, 0, 2)), hgrn_norm_gain,
      conv_norm_gain, w_out, final_norm_gain.reshape(1, d_model))
```

```python
import jax
import jax.numpy as jnp
from jax import lax
from jax.experimental import pallas as pl
from jax.experimental.pallas import tpu as pltpu

D_HGRN = 512
D_CONV = 512
HEAD_DIM = 128
N_HEADS = D_HGRN // HEAD_DIM
CHUNK = 64
BLOCK = 2 * CHUNK
CONV_WIDTH = 3
CONV_GROUP_DIM = 64
EPS = 1e-6
SEQ_TILE = 1024
SUB_TILE = 1024
LANES = 128
SUBLANES = 8
VMEM_LIMIT_BYTES = 60000 * 1024
WEIGHT_SLAB = 512

_NT = (((1,), (1,)), ((), ()))
_TN = (((0,), (0,)), ((), ()))


def _silu(z):
    hz = 0.5 * z
    return hz + hz * jnp.tanh(hz)


def _slab(x, r0, lb, g_ref, win_ref, cw_ref, hg_ref, cg_ref, wout_ref, fg_ref,
          st_ref, cu_ref, o_s, loader=None):
    f32, bf16 = jnp.float32, jnp.bfloat16
    n_rows = x.shape[0]

    ms = jnp.mean(x * x, axis=-1, keepdims=True)
    h = (x * lax.rsqrt(ms + EPS) * g_ref[...]).astype(bf16)

    def proj(j, width):
        if loader is not None:
            loader.need_w_in(j)
        return jnp.dot(h, win_ref[:, j * width:(j + 1) * width], preferred_element_type=f32)

    q = proj(0, D_HGRN)
    f = (0.5 + 0.5 * lb) + (0.5 - 0.5 * lb) * jnp.tanh(0.5 * proj(1, D_HGRN))
    lf = jnp.log(f)
    k = 1.0 - f
    v = proj(2, D_HGRN).astype(bf16)
    u = proj(4, D_CONV)

    row = lax.broadcasted_iota(jnp.int32, (BLOCK, BLOCK), 0)
    col = lax.broadcasted_iota(jnp.int32, (BLOCK, BLOCK), 1)
    causal = row >= col
    rel = (((col >= CHUNK) & (col <= row)).astype(f32)
           - ((col < CHUNK) & (col > row)).astype(f32)).astype(bf16)
    rel2 = jnp.concatenate([rel, rel], axis=1)
    n_blocks = n_rows // BLOCK
    rows = [slice(c * BLOCK, (c + 1) * BLOCK) for c in range(n_blocks)]
    heads = [slice(hd * HEAD_DIM, (hd + 1) * HEAD_DIM) for hd in range(N_HEADS)]

    lf_hi = lf.astype(bf16)
    lf_lo = (lf - lf_hi.astype(f32)).astype(bf16)
    b = jnp.concatenate(
        [jnp.dot(rel2, jnp.concatenate([lf_hi[r], lf_lo[r]], axis=0),
                 preferred_element_type=f32) for r in rows], axis=0)
    gate_c = proj(6, D_CONV)
    eb = jnp.exp(b)
    eib = jnp.exp(-b)
    q_dec = (q * eb).astype(bf16)
    k_inv = k * eib
    head = [eib[r][0:1, :] * f[r][0:1, :] for r in rows]
    tail = [eb[r][BLOCK - 1:BLOCK, :] for r in rows]
    k_end = [(k_inv[r] * tail[c]).astype(bf16) for c, r in enumerate(rows)]
    k_inv = k_inv.astype(bf16)

    scores = [[lax.dot_general(q_dec[r, sl], k_inv[r, sl], _NT, preferred_element_type=f32)
               for sl in heads] for r in rows]
    upd_t = [[lax.dot_general(v[r, sl], k_end[c][:, sl], _TN, preferred_element_type=f32)
              for sl in heads] for c, r in enumerate(rows)]
    gate_b = proj(5, D_CONV)
    state_before = [[None] * N_HEADS for _ in rows]
    for hd, sl in enumerate(heads):
        state_t = st_ref[hd]
        for c in range(n_blocks):
            state_before[c][hd] = (state_t * head[c][:, sl]).T.astype(bf16)
            state_t = state_t * (head[c][:, sl] * tail[c][:, sl]) + upd_t[c][hd]
        st_ref[hd] = state_t
    for c, r in enumerate(rows):
        for hd, sl in enumerate(heads):
            sc = jnp.where(causal, scores[c][hd], 0.0).astype(bf16)
            o_s[r0 + c * BLOCK:r0 + (c + 1) * BLOCK, sl] = (
                jnp.dot(sc, v[r, sl], preferred_element_type=f32)
                + jnp.dot(q_dec[r, sl], state_before[c][hd], preferred_element_type=f32))
    z_b = proj(7, D_CONV)
    z_a = proj(3, D_HGRN)

    hg = hg_ref[...]
    o_parts = []
    for sl in heads:
        oh = o_s[r0:r0 + n_rows, sl]
        oms = jnp.mean(oh * oh, axis=-1, keepdims=True)
        o_parts.append(oh * lax.rsqrt(oms + EPS) * hg[:, sl])
    o_a = (jnp.concatenate(o_parts, axis=-1) * _silu(z_a)).astype(bf16)

    cw = [cw_ref[j] for j in range(CONV_WIDTH)]
    cu = gate_c * u
    c0 = r0 + SUBLANES
    cu_ref[c0:c0 + n_rows, :] = cu
    conv = (cw[2] * cu
            + cw[1] * cu_ref[c0 - 1:c0 - 1 + n_rows, :]
            + cw[0] * cu_ref[c0 - 2:c0 - 2 + n_rows, :])
    y = gate_b * conv
    cg = cg_ref[...]
    lane = lax.broadcasted_iota(jnp.int32, (1, LANES), 1)
    first_group = lane < CONV_GROUP_DIM
    y_parts = []
    for j in range(D_CONV // LANES):
        sl = slice(j * LANES, (j + 1) * LANES)
        yj = y[:, sl]
        y2 = yj * yj
        tot = jnp.sum(y2, axis=-1, keepdims=True)
        lo = jnp.sum(jnp.where(first_group, y2, 0.0), axis=-1, keepdims=True)
        yms = jnp.where(first_group, lo, tot - lo) * (1.0 / CONV_GROUP_DIM)
        y_parts.append(yj * lax.rsqrt(yms + EPS) * cg[:, sl])
    o_b = (jnp.concatenate(y_parts, axis=-1) * _silu(z_b)).astype(bf16)

    if loader is not None:
        loader.need_all()
    mix = (jnp.dot(o_a, wout_ref[0:D_HGRN, :], preferred_element_type=f32)
           + jnp.dot(o_b, wout_ref[D_HGRN:D_HGRN + D_CONV, :], preferred_element_type=f32))
    res = x + mix
    rms = jnp.mean(res * res, axis=-1, keepdims=True)
    return res * lax.rsqrt(rms + EPS) * fg_ref[...]


W_IN_USE_ORDER = (0, 1, 2, 4, 6, 5, 7, 3)


class _WeightLoader:
    def __init__(self, win_hbm, wout_hbm, win_ref, wout_ref, stage, sems):
        assert win_ref.shape[1] == len(W_IN_USE_ORDER) * WEIGHT_SLAB
        cols = lambda j: pl.ds(j * WEIGHT_SLAB, WEIGHT_SLAB)
        self.slabs = [(win_hbm.at[0, :, cols(j)], win_ref.at[:, cols(j)]) for j in W_IN_USE_ORDER]
        self.slabs += [(wout_hbm.at[0, :, cols(j)], wout_ref.at[:, cols(j)])
                       for j in range(wout_ref.shape[1] // WEIGHT_SLAB)]
        self.stage, self.sems = stage, sems
        self.n_ready = 0
        for i in range(min(2, len(self.slabs))):
            self._copy(i).start()

    def _copy(self, i):
        return pltpu.make_async_copy(self.slabs[i][0], self.stage.at[i % 2], self.sems.at[i % 2])

    def _need(self, n):
        while self.n_ready < n:
            i = self.n_ready
            self._copy(i).wait()
            self.slabs[i][1][...] = self.stage[i % 2].astype(jnp.bfloat16)
            if i + 2 < len(self.slabs):
                self._copy(i + 2).start()
            self.n_ready += 1

    def need_w_in(self, j):
        self._need(W_IN_USE_ORDER.index(j) + 1)

    def need_all(self):
        self._need(len(self.slabs))


def _block_kernel(x_ref, g_ref, win_hbm, lbl_ref, cw_ref, hg_ref, cg_ref, wout_hbm, fg_ref,
                  out_ref, st_ref, cu_ref, o_s, win_ref, wout_ref, stage, sems):
    tile = x_ref.shape[0]
    sub = min(SUB_TILE, tile)

    @pl.when(pl.program_id(1) == 0)
    def _reset_carries():
        st_ref[...] = jnp.zeros_like(st_ref)
        cu_ref[0:SUBLANES, :] = jnp.zeros((SUBLANES, D_CONV), jnp.float32)

    def run_step(loader):
        lbl = lbl_ref[...]
        lmax = jnp.max(lbl, axis=0, keepdims=True)
        lexp = jnp.exp(lbl - lmax)
        lb = lexp[0:1, :] / jnp.sum(lexp, axis=0, keepdims=True)
        for r0 in range(0, tile, sub):
            out_ref[r0:r0 + sub, :] = _slab(
                x_ref[r0:r0 + sub, :], r0, lb, g_ref, win_ref, cw_ref, hg_ref, cg_ref, wout_ref,
                fg_ref, st_ref, cu_ref, o_s, loader if r0 == 0 else None)
        cu_ref[0:SUBLANES, :] = cu_ref[tile:tile + SUBLANES, :]

    first_step = (pl.program_id(0) == 0) & (pl.program_id(1) == 0)

    @pl.when(first_step)
    def _step_with_weight_load():
        run_step(_WeightLoader(win_hbm, wout_hbm, win_ref, wout_ref, stage, sems))

    @pl.when(jnp.logical_not(first_step))
    def _step():
        run_step(None)


@jax.jit
def kernel(x, norm_gain, w_in, lb_logits, conv_w, hgrn_norm_gain, conv_norm_gain, w_out,
           final_norm_gain):
    bsz, seq, d_model = x.shape
    depth = norm_gain.shape[0]
    assert depth == 1 and lb_logits.shape == (depth + 1, D_HGRN)
    assert w_in.shape == (depth, d_model, 4 * D_HGRN + 4 * D_CONV)
    assert w_out.shape == (depth, D_HGRN + D_CONV, d_model)
    tile = min(SEQ_TILE, seq)
    sub = min(SUB_TILE, tile)
    assert seq % tile == 0 and tile % sub == 0 and sub % BLOCK == 0
    in_cols = w_in.shape[2]
    assert in_cols % WEIGHT_SLAB == 0 and d_model % WEIGHT_SLAB == 0

    full = lambda shape: pl.BlockSpec(shape, lambda b, s: (0,) * len(shape))
    grid_spec = pltpu.PrefetchScalarGridSpec(
        num_scalar_prefetch=0,
        grid=(bsz, seq // tile),
        in_specs=[
            pl.BlockSpec((None, tile, d_model), lambda b, s: (b, s, 0)),
            full((1, d_model)),
            pl.BlockSpec(memory_space=pltpu.HBM),
            full((depth + 1, D_HGRN)),
            full((CONV_WIDTH, 1, D_CONV)),
            full((1, D_HGRN)),
            full((1, D_CONV)),
            pl.BlockSpec(memory_space=pltpu.HBM),
            full((1, d_model)),
        ],
        out_specs=pl.BlockSpec((None, tile, d_model), lambda b, s: (b, s, 0)),
        scratch_shapes=[
            pltpu.VMEM((N_HEADS, HEAD_DIM, HEAD_DIM), jnp.float32),
            pltpu.VMEM((tile + SUBLANES, D_CONV), jnp.float32),
            pltpu.VMEM((tile, D_HGRN), jnp.float32),
            pltpu.VMEM((d_model, in_cols), jnp.bfloat16),
            pltpu.VMEM((D_HGRN + D_CONV, d_model), jnp.bfloat16),
            pltpu.VMEM((2, d_model, WEIGHT_SLAB), jnp.float32),
            pltpu.SemaphoreType.DMA((2,)),
        ],
    )
    return pl.pallas_call(
        _block_kernel,
        grid_spec=grid_spec,
        out_shape=jax.ShapeDtypeStruct(x.shape, x.dtype),
        compiler_params=pltpu.CompilerParams(
            dimension_semantics=("arbitrary", "arbitrary"),
            vmem_limit_bytes=VMEM_LIMIT_BYTES),
        name="hgrn2_shortconv_block",
    )(x, norm_gain, w_in, lb_logits, jnp.transpose(conv_w, (1, 0, 2)), hgrn_norm_gain,
      conv_norm_gain, w_out, final_norm_gain.reshape(1, d_model))
```

```python
import jax
import jax.numpy as jnp
from jax import lax
from jax.experimental import pallas as pl
from jax.experimental.pallas import tpu as pltpu

D_HGRN = 512
D_CONV = 512
HEAD_DIM = 128
N_HEADS = D_HGRN // HEAD_DIM
CHUNK = 64
BLOCK = 2 * CHUNK
CONV_WIDTH = 3
CONV_GROUP_DIM = 64
EPS = 1e-6
SEQ_TILE = 1024
SUB_TILE = 1024
LANES = 128
SUBLANES = 8
VMEM_LIMIT_BYTES = 60000 * 1024
WEIGHT_SLAB = 512

_NT = (((1,), (1,)), ((), ()))
_TN = (((0,), (0,)), ((), ()))


def _silu(z):
    hz = 0.5 * z
    return hz + hz * jnp.tanh(hz)


def _slab(x, r0, lb, g_ref, win_ref, cw_ref, hg_ref, cg_ref, wout_ref, fg_ref,
          st_ref, cu_ref, o_s, loader=None):
    f32, bf16 = jnp.float32, jnp.bfloat16
    n_rows = x.shape[0]

    ms = jnp.mean(x * x, axis=-1, keepdims=True)
    h = (x * lax.rsqrt(ms + EPS) * g_ref[...]).astype(bf16)

    def proj(j, width):
        if loader is not None:
            loader.need_w_in(j)
        return jnp.dot(h, win_ref[:, j * width:(j + 1) * width], preferred_element_type=f32)

    q = proj(0, D_HGRN)
    f = (0.5 + 0.5 * lb) + (0.5 - 0.5 * lb) * jnp.tanh(0.5 * proj(1, D_HGRN))
    lf = jnp.log(f)
    k = 1.0 - f
    v = proj(2, D_HGRN).astype(bf16)
    u = proj(4, D_CONV)

    row = lax.broadcasted_iota(jnp.int32, (BLOCK, BLOCK), 0)
    col = lax.broadcasted_iota(jnp.int32, (BLOCK, BLOCK), 1)
    causal = row >= col
    rel = (((col >= CHUNK) & (col <= row)).astype(f32)
           - ((col < CHUNK) & (col > row)).astype(f32)).astype(bf16)
    rel2 = jnp.concatenate([rel, rel], axis=1)
    n_blocks = n_rows // BLOCK
    rows = [slice(c * BLOCK, (c + 1) * BLOCK) for c in range(n_blocks)]
    heads = [slice(hd * HEAD_DIM, (hd + 1) * HEAD_DIM) for hd in range(N_HEADS)]

    lf_hi = lf.astype(bf16)
    lf_lo = (lf - lf_hi.astype(f32)).astype(bf16)
    b = jnp.concatenate(
        [jnp.dot(rel2, jnp.concatenate([lf_hi[r], lf_lo[r]], axis=0),
                 preferred_element_type=f32) for r in rows], axis=0)
    gate_c = proj(6, D_CONV)
    eb = jnp.exp(b)
    eib = jnp.exp(-b)
    q_dec = (q * eb).astype(bf16)
    k_inv = k * eib
    head = [eib[r][0:1, :] * f[r][0:1, :] for r in rows]
    tail = [eb[r][BLOCK - 1:BLOCK, :] for r in rows]
    k_end = [(k_inv[r] * tail[c]).astype(bf16) for c, r in enumerate(rows)]
    k_inv = k_inv.astype(bf16)

    scores = [[lax.dot_general(q_dec[r, sl], k_inv[r, sl], _NT, preferred_element_type=f32)
               for sl in heads] for r in rows]
    upd_t = [[lax.dot_general(v[r, sl], k_end[c][:, sl], _TN, preferred_element_type=f32)
              for sl in heads] for c, r in enumerate(rows)]
    gate_b = proj(5, D_CONV)
    state_before = [[None] * N_HEADS for _ in rows]
    for hd, sl in enumerate(heads):
        state_t = st_ref[hd]
        for c in range(n_blocks):
            state_before[c][hd] = (state_t * head[c][:, sl]).astype(bf16).T
            state_t = state_t * (head[c][:, sl] * tail[c][:, sl]) + upd_t[c][hd]
        st_ref[hd] = state_t
    for c, r in enumerate(rows):
        for hd, sl in enumerate(heads):
            sc = jnp.where(causal, scores[c][hd], 0.0).astype(bf16)
            o_s[r0 + c * BLOCK:r0 + (c + 1) * BLOCK, sl] = (
                jnp.dot(sc, v[r, sl], preferred_element_type=f32)
                + jnp.dot(q_dec[r, sl], state_before[c][hd], preferred_element_type=f32))
    z_b = proj(7, D_CONV)
    z_a = proj(3, D_HGRN)

    hg = hg_ref[...]
    o_parts = []
    for sl in heads:
        oh = o_s[r0:r0 + n_rows, sl]
        oms = jnp.mean(oh * oh, axis=-1, keepdims=True)
        o_parts.append(oh * lax.rsqrt(oms + EPS) * hg[:, sl])
    o_a = (jnp.concatenate(o_parts, axis=-1) * _silu(z_a)).astype(bf16)

    cw = [cw_ref[j] for j in range(CONV_WIDTH)]
    cu = gate_c * u
    c0 = r0 + SUBLANES
    cu_ref[c0:c0 + n_rows, :] = cu
    conv = (cw[2] * cu
            + cw[1] * cu_ref[c0 - 1:c0 - 1 + n_rows, :]
            + cw[0] * cu_ref[c0 - 2:c0 - 2 + n_rows, :])
    y = gate_b * conv
    cg = cg_ref[...]
    lane = lax.broadcasted_iota(jnp.int32, (1, LANES), 1)
    first_group = lane < CONV_GROUP_DIM
    y_parts = []
    for j in range(D_CONV // LANES):
        sl = slice(j * LANES, (j + 1) * LANES)
        yj = y[:, sl]
        y2 = yj * yj
        tot = jnp.sum(y2, axis=-1, keepdims=True)
        lo = jnp.sum(jnp.where(first_group, y2, 0.0), axis=-1, keepdims=True)
        yms = jnp.where(first_group, lo, tot - lo) * (1.0 / CONV_GROUP_DIM)
        y_parts.append(yj * lax.rsqrt(yms + EPS) * cg[:, sl])
    o_b = (jnp.concatenate(y_parts, axis=-1) * _silu(z_b)).astype(bf16)

    if loader is not None:
        loader.need_all()
    mix = (jnp.dot(o_a, wout_ref[0:D_HGRN, :], preferred_element_type=f32)
           + jnp.dot(o_b, wout_ref[D_HGRN:D_HGRN + D_CONV, :], preferred_element_type=f32))
    res = x + mix
    rms = jnp.mean(res * res, axis=-1, keepdims=True)
    return res * lax.rsqrt(rms + EPS) * fg_ref[...]


W_IN_USE_ORDER = (0, 1, 2, 4, 6, 5, 7, 3)


class _WeightLoader:
    def __init__(self, win_hbm, wout_hbm, win_ref, wout_ref, stage, sems):
        assert win_ref.shape[1] == len(W_IN_USE_ORDER) * WEIGHT_SLAB
        cols = lambda j: pl.ds(j * WEIGHT_SLAB, WEIGHT_SLAB)
        self.slabs = [(win_hbm.at[0, :, cols(j)], win_ref.at[:, cols(j)]) for j in W_IN_USE_ORDER]
        self.slabs += [(wout_hbm.at[0, :, cols(j)], wout_ref.at[:, cols(j)])
                       for j in range(wout_ref.shape[1] // WEIGHT_SLAB)]
        self.stage, self.sems = stage, sems
        self.n_ready = 0
        for i in range(min(2, len(self.slabs))):
            self._copy(i).start()

    def _copy(self, i):
        return pltpu.make_async_copy(self.slabs[i][0], self.stage.at[i % 2], self.sems.at[i % 2])

    def _need(self, n):
        while self.n_ready < n:
            i = self.n_ready
            self._copy(i).wait()
            self.slabs[i][1][...] = self.stage[i % 2].astype(jnp.bfloat16)
            if i + 2 < len(self.slabs):
                self._copy(i + 2).start()
            self.n_ready += 1

    def need_w_in(self, j):
        self._need(W_IN_USE_ORDER.index(j) + 1)

    def need_all(self):
        self._need(len(self.slabs))


def _block_kernel(x_ref, g_ref, win_hbm, lbl_ref, cw_ref, hg_ref, cg_ref, wout_hbm, fg_ref,
                  out_ref, st_ref, cu_ref, o_s, win_ref, wout_ref, stage, sems):
    tile = x_ref.shape[0]
    sub = min(SUB_TILE, tile)

    @pl.when(pl.program_id(1) == 0)
    def _reset_carries():
        st_ref[...] = jnp.zeros_like(st_ref)
        cu_ref[0:SUBLANES, :] = jnp.zeros((SUBLANES, D_CONV), jnp.float32)

    def run_step(loader):
        lbl = lbl_ref[...]
        lmax = jnp.max(lbl, axis=0, keepdims=True)
        lexp = jnp.exp(lbl - lmax)
        lb = lexp[0:1, :] / jnp.sum(lexp, axis=0, keepdims=True)
        for r0 in range(0, tile, sub):
            out_ref[r0:r0 + sub, :] = _slab(
                x_ref[r0:r0 + sub, :], r0, lb, g_ref, win_ref, cw_ref, hg_ref, cg_ref, wout_ref,
                fg_ref, st_ref, cu_ref, o_s, loader if r0 == 0 else None)
        cu_ref[0:SUBLANES, :] = cu_ref[tile:tile + SUBLANES, :]

    first_step = (pl.program_id(0) == 0) & (pl.program_id(1) == 0)

    @pl.when(first_step)
    def _step_with_weight_load():
        run_step(_WeightLoader(win_hbm, wout_hbm, win_ref, wout_ref, stage, sems))

    @pl.when(jnp.logical_not(first_step))
    def _step():
        run_step(None)


@jax.jit
def kernel(x, norm_gain, w_in, lb_logits, conv_w, hgrn_norm_gain, conv_norm_gain, w_out,
           final_norm_gain):
    bsz, seq, d_model = x.shape
    depth = norm_gain.shape[0]
    assert depth == 1 and lb_logits.shape == (depth + 1, D_HGRN)
    assert w_in.shape == (depth, d_model, 4 * D_HGRN + 4 * D_CONV)
    assert w_out.shape == (depth, D_HGRN + D_CONV, d_model)
    tile = min(SEQ_TILE, seq)
    sub = min(SUB_TILE, tile)
    assert seq % tile == 0 and tile % sub == 0 and sub % BLOCK == 0
    in_cols = w_in.shape[2]
    assert in_cols % WEIGHT_SLAB == 0 and d_model % WEIGHT_SLAB == 0

    full = lambda shape: pl.BlockSpec(shape, lambda b, s: (0,) * len(shape))
    grid_spec = pltpu.PrefetchScalarGridSpec(
        num_scalar_prefetch=0,
        grid=(bsz, seq // tile),
        in_specs=[
            pl.BlockSpec((None, tile, d_model), lambda b, s: (b, s, 0)),
            full((1, d_model)),
            pl.BlockSpec(memory_space=pltpu.HBM),
            full((depth + 1, D_HGRN)),
            full((CONV_WIDTH, 1, D_CONV)),
            full((1, D_HGRN)),
            full((1, D_CONV)),
            pl.BlockSpec(memory_space=pltpu.HBM),
            full((1, d_model)),
        ],
        out_specs=pl.BlockSpec((None, tile, d_model), lambda b, s: (b, s, 0)),
        scratch_shapes=[
            pltpu.VMEM((N_HEADS, HEAD_DIM, HEAD_DIM), jnp.float32),
            pltpu.VMEM((tile + SUBLANES, D_CONV), jnp.float32),
            pltpu.VMEM((tile, D_HGRN), jnp.float32),
            pltpu.VMEM((d_model, in_cols), jnp.bfloat16),
            pltpu.VMEM((D_HGRN + D_CONV, d_model), jnp.bfloat16),
            pltpu.VMEM((2, d_model, WEIGHT_SLAB), jnp.float32),
            pltpu.SemaphoreType.DMA((2,)),
        ],
    )
    return pl.pallas_call(
        _block_kernel,
        grid_spec=grid_spec,
        out_shape=jax.ShapeDtypeStruct(x.shape, x.dtype),
        compiler_params=pltpu.CompilerParams(
            dimension_semantics=("arbitrary", "arbitrary"),
            vmem_limit_bytes=VMEM_LIMIT_BYTES),
        name="hgrn2_shortconv_block",
    )(x, norm_gain, w_in, lb_logits, jnp.transpose(conv_w, (1, 0, 2)), hgrn_norm_gain,
      conv_norm_gain, w_out, final_norm_gain.reshape(1, d_model))
```

```python
import jax
import jax.numpy as jnp
from jax import lax
from jax.experimental import pallas as pl
from jax.experimental.pallas import tpu as pltpu

D_HGRN = 512
D_CONV = 512
HEAD_DIM = 128
N_HEADS = D_HGRN // HEAD_DIM
CHUNK = 64
BLOCK = 2 * CHUNK
CONV_WIDTH = 3
CONV_GROUP_DIM = 64
EPS = 1e-6
SEQ_TILE = 1024
SUB_TILE = 1024
LANES = 128
SUBLANES = 8
VMEM_LIMIT_BYTES = 60000 * 1024
WEIGHT_SLAB = 512

_NT = (((1,), (1,)), ((), ()))
_TN = (((0,), (0,)), ((), ()))


def _silu(z):
    hz = 0.5 * z
    return hz + hz * jnp.tanh(hz)


def _slab(x, r0, lb, g_ref, win_ref, cw_ref, hg_ref, cg_ref, wout_ref, fg_ref,
          st_ref, cu_ref, o_s, loader=None):
    f32, bf16 = jnp.float32, jnp.bfloat16
    n_rows = x.shape[0]

    ms = jnp.mean(x * x, axis=-1, keepdims=True)
    h = (x * lax.rsqrt(ms + EPS) * g_ref[...]).astype(bf16)

    def proj(j, width):
        if loader is not None:
            loader.need_w_in(j)
        return jnp.dot(h, win_ref[:, j * width:(j + 1) * width], preferred_element_type=f32)

    q = proj(0, D_HGRN)
    f = (0.5 + 0.5 * lb) + (0.5 - 0.5 * lb) * jnp.tanh(0.5 * proj(1, D_HGRN))
    lf = jnp.log(f)
    k = 1.0 - f
    v = proj(2, D_HGRN).astype(bf16)
    u = proj(4, D_CONV)

    row = lax.broadcasted_iota(jnp.int32, (BLOCK, BLOCK), 0)
    col = lax.broadcasted_iota(jnp.int32, (BLOCK, BLOCK), 1)
    causal = row >= col
    rel = (((col >= CHUNK) & (col <= row)).astype(f32)
           - ((col < CHUNK) & (col > row)).astype(f32)).astype(bf16)
    rel2 = jnp.concatenate([rel, rel], axis=1)
    n_blocks = n_rows // BLOCK
    rows = [slice(c * BLOCK, (c + 1) * BLOCK) for c in range(n_blocks)]
    heads = [slice(hd * HEAD_DIM, (hd + 1) * HEAD_DIM) for hd in range(N_HEADS)]

    lf_hi = lf.astype(bf16)
    lf_lo = (lf - lf_hi.astype(f32)).astype(bf16)
    b = jnp.concatenate(
        [jnp.dot(rel2, jnp.concatenate([lf_hi[r], lf_lo[r]], axis=0),
                 preferred_element_type=f32) for r in rows], axis=0)
    gate_c = proj(6, D_CONV)
    eb = jnp.exp(b)
    eib = jnp.exp(-b)
    q_dec = (q * eb).astype(bf16)
    k_inv = k * eib
    head = [eib[r][0:1, :] * f[r][0:1, :] for r in rows]
    tail = [eb[r][BLOCK - 1:BLOCK, :] for r in rows]
    k_end = [(k_inv[r] * tail[c]).astype(bf16) for c, r in enumerate(rows)]

    scores = [[jnp.dot(q_dec[r, sl], k_inv[r, sl].T.astype(bf16), preferred_element_type=f32)
               for sl in heads] for r in rows]
    upd_t = [[lax.dot_general(v[r, sl], k_end[c][:, sl], _TN, preferred_element_type=f32)
              for sl in heads] for c, r in enumerate(rows)]
    gate_b = proj(5, D_CONV)
    state_before = [[None] * N_HEADS for _ in rows]
    for hd, sl in enumerate(heads):
        state_t = st_ref[hd]
        for c in range(n_blocks):
            state_before[c][hd] = (state_t * head[c][:, sl]).T.astype(bf16)
            state_t = state_t * (head[c][:, sl] * tail[c][:, sl]) + upd_t[c][hd]
        st_ref[hd] = state_t
    for c, r in enumerate(rows):
        for hd, sl in enumerate(heads):
            sc = jnp.where(causal, scores[c][hd], 0.0).astype(bf16)
            o_s[r0 + c * BLOCK:r0 + (c + 1) * BLOCK, sl] = (
                jnp.dot(sc, v[r, sl], preferred_element_type=f32)
                + jnp.dot(q_dec[r, sl], state_before[c][hd], preferred_element_type=f32))
    z_b = proj(7, D_CONV)
    z_a = proj(3, D_HGRN)

    hg = hg_ref[...]
    o_parts = []
    for sl in heads:
        oh = o_s[r0:r0 + n_rows, sl]
        oms = jnp.mean(oh * oh, axis=-1, keepdims=True)
        o_parts.append(oh * lax.rsqrt(oms + EPS) * hg[:, sl])
    o_a = (jnp.concatenate(o_parts, axis=-1) * _silu(z_a)).astype(bf16)

    cw = [cw_ref[j] for j in range(CONV_WIDTH)]
    cu = gate_c * u
    c0 = r0 + SUBLANES
    cu_ref[c0:c0 + n_rows, :] = cu
    conv = (cw[2] * cu
            + cw[1] * cu_ref[c0 - 1:c0 - 1 + n_rows, :]
            + cw[0] * cu_ref[c0 - 2:c0 - 2 + n_rows, :])
    y = gate_b * conv
    cg = cg_ref[...]
    lane = lax.broadcasted_iota(jnp.int32, (1, LANES), 1)
    first_group = lane < CONV_GROUP_DIM
    y_parts = []
    for j in range(D_CONV // LANES):
        sl = slice(j * LANES, (j + 1) * LANES)
        yj = y[:, sl]
        y2 = yj * yj
        tot = jnp.sum(y2, axis=-1, keepdims=True)
        lo = jnp.sum(jnp.where(first_group, y2, 0.0), axis=-1, keepdims=True)
        yms = jnp.where(first_group, lo, tot - lo) * (1.0 / CONV_GROUP_DIM)
        y_parts.append(yj * lax.rsqrt(yms + EPS) * cg[:, sl])
    o_b = (jnp.concatenate(y_parts, axis=-1) * _silu(z_b)).astype(bf16)

    if loader is not None:
        loader.need_all()
    mix = (jnp.dot(o_a, wout_ref[0:D_HGRN, :], preferred_element_type=f32)
           + jnp.dot(o_b, wout_ref[D_HGRN:D_HGRN + D_CONV, :], preferred_element_type=f32))
    res = x + mix
    rms = jnp.mean(res * res, axis=-1, keepdims=True)
    return res * lax.rsqrt(rms + EPS) * fg_ref[...]


W_IN_USE_ORDER = (0, 1, 2, 4, 6, 5, 7, 3)


class _WeightLoader:
    def __init__(self, win_hbm, wout_hbm, win_ref, wout_ref, stage, sems):
        assert win_ref.shape[1] == len(W_IN_USE_ORDER) * WEIGHT_SLAB
        cols = lambda j: pl.ds(j * WEIGHT_SLAB, WEIGHT_SLAB)
        self.slabs = [(win_hbm.at[0, :, cols(j)], win_ref.at[:, cols(j)]) for j in W_IN_USE_ORDER]
        self.slabs += [(wout_hbm.at[0, :, cols(j)], wout_ref.at[:, cols(j)])
                       for j in range(wout_ref.shape[1] // WEIGHT_SLAB)]
        self.stage, self.sems = stage, sems
        self.n_ready = 0
        for i in range(min(2, len(self.slabs))):
            self._copy(i).start()

    def _copy(self, i):
        return pltpu.make_async_copy(self.slabs[i][0], self.stage.at[i % 2], self.sems.at[i % 2])

    def _need(self, n):
        while self.n_ready < n:
            i = self.n_ready
            self._copy(i).wait()
            self.slabs[i][1][...] = self.stage[i % 2].astype(jnp.bfloat16)
            if i + 2 < len(self.slabs):
                self._copy(i + 2).start()
            self.n_ready += 1

    def need_w_in(self, j):
        self._need(W_IN_USE_ORDER.index(j) + 1)

    def need_all(self):
        self._need(len(self.slabs))


def _block_kernel(x_ref, g_ref, win_hbm, lbl_ref, cw_ref, hg_ref, cg_ref, wout_hbm, fg_ref,
                  out_ref, st_ref, cu_ref, o_s, win_ref, wout_ref, stage, sems):
    tile = x_ref.shape[0]
    sub = min(SUB_TILE, tile)

    @pl.when(pl.program_id(1) == 0)
    def _reset_carries():
        st_ref[...] = jnp.zeros_like(st_ref)
        cu_ref[0:SUBLANES, :] = jnp.zeros((SUBLANES, D_CONV), jnp.float32)

    def run_step(loader):
        lbl = lbl_ref[...]
        lmax = jnp.max(lbl, axis=0, keepdims=True)
        lexp = jnp.exp(lbl - lmax)
        lb = lexp[0:1, :] / jnp.sum(lexp, axis=0, keepdims=True)
        for r0 in range(0, tile, sub):
            out_ref[r0:r0 + sub, :] = _slab(
                x_ref[r0:r0 + sub, :], r0, lb, g_ref, win_ref, cw_ref, hg_ref, cg_ref, wout_ref,
                fg_ref, st_ref, cu_ref, o_s, loader if r0 == 0 else None)
        cu_ref[0:SUBLANES, :] = cu_ref[tile:tile + SUBLANES, :]

    first_step = (pl.program_id(0) == 0) & (pl.program_id(1) == 0)

    @pl.when(first_step)
    def _step_with_weight_load():
        run_step(_WeightLoader(win_hbm, wout_hbm, win_ref, wout_ref, stage, sems))

    @pl.when(jnp.logical_not(first_step))
    def _step():
        run_step(None)


@jax.jit
def kernel(x, norm_gain, w_in, lb_logits, conv_w, hgrn_norm_gain, conv_norm_gain, w_out,
           final_norm_gain):
    bsz, seq, d_model = x.shape
    depth = norm_gain.shape[0]
    assert depth == 1 and lb_logits.shape == (depth + 1, D_HGRN)
    assert w_in.shape == (depth, d_model, 4 * D_HGRN + 4 * D_CONV)
    assert w_out.shape == (depth, D_HGRN + D_CONV, d_model)
    tile = min(SEQ_TILE, seq)
    sub = min(SUB_TILE, tile)
    assert seq % tile == 0 and tile % sub == 0 and sub % BLOCK == 0
    in_cols = w_in.shape[2]
    assert in_cols % WEIGHT_SLAB == 0 and d_model % WEIGHT_SLAB == 0

    full = lambda shape: pl.BlockSpec(shape, lambda b, s: (0,) * len(shape))
    grid_spec = pltpu.PrefetchScalarGridSpec(
        num_scalar_prefetch=0,
        grid=(bsz, seq // tile),
        in_specs=[
            pl.BlockSpec((None, tile, d_model), lambda b, s: (b, s, 0)),
            full((1, d_model)),
            pl.BlockSpec(memory_space=pltpu.HBM),
            full((depth + 1, D_HGRN)),
            full((CONV_WIDTH, 1, D_CONV)),
            full((1, D_HGRN)),
            full((1, D_CONV)),
            pl.BlockSpec(memory_space=pltpu.HBM),
            full((1, d_model)),
        ],
        out_specs=pl.BlockSpec((None, tile, d_model), lambda b, s: (b, s, 0)),
        scratch_shapes=[
            pltpu.VMEM((N_HEADS, HEAD_DIM, HEAD_DIM), jnp.float32),
            pltpu.VMEM((tile + SUBLANES, D_CONV), jnp.float32),
            pltpu.VMEM((tile, D_HGRN), jnp.float32),
            pltpu.VMEM((d_model, in_cols), jnp.bfloat16),
            pltpu.VMEM((D_HGRN + D_CONV, d_model), jnp.bfloat16),
            pltpu.VMEM((2, d_model, WEIGHT_SLAB), jnp.float32),
            pltpu.SemaphoreType.DMA((2,)),
        ],
    )
    return pl.pallas_call(
        _block_kernel,
        grid_spec=grid_spec,
        out_shape=jax.ShapeDtypeStruct(x.shape, x.dtype),
        compiler_params=pltpu.CompilerParams(
            dimension_semantics=("arbitrary", "arbitrary"),
            vmem_limit_bytes=VMEM_LIMIT_BYTES),
        name="hgrn2_shortconv_block",
    )(x, norm_gain, w_in, lb_logits, jnp.transpose(conv_w, (1, 0, 2)), hgrn_norm_gain,
      conv_norm_gain, w_out, final_norm_gain.reshape(1, d_model))
```

```python
import jax
import jax.numpy as jnp
from jax import lax
from jax.experimental import pallas as pl
from jax.experimental.pallas import tpu as pltpu

D_HGRN = 512
D_CONV = 512
HEAD_DIM = 128
N_HEADS = D_HGRN // HEAD_DIM
CHUNK = 64
BLOCK = 2 * CHUNK
CONV_WIDTH = 3
CONV_GROUP_DIM = 64
EPS = 1e-6
SEQ_TILE = 1024
SUB_TILE = 1024
LANES = 128
SUBLANES = 8
VMEM_LIMIT_BYTES = 60000 * 1024
WEIGHT_SLAB = 512

_NT = (((1,), (1,)), ((), ()))
_TN = (((0,), (0,)), ((), ()))


def _silu(z):
    hz = 0.5 * z
    return hz + hz * jnp.tanh(hz)


def _slab(x, r0, lb, g_ref, win_ref, cw_ref, hg_ref, cg_ref, wout_ref, fg_ref,
          st_ref, cu_ref, o_s, loader=None):
    f32, bf16 = jnp.float32, jnp.bfloat16
    n_rows = x.shape[0]

    ms = jnp.mean(x * x, axis=-1, keepdims=True)
    h = (x * lax.rsqrt(ms + EPS) * g_ref[...]).astype(bf16)

    def proj(j, width):
        if loader is not None:
            loader.need_w_in(j)
        return jnp.dot(h, win_ref[:, j * width:(j + 1) * width], preferred_element_type=f32)

    q = proj(0, D_HGRN)
    f = (0.5 + 0.5 * lb) + (0.5 - 0.5 * lb) * jnp.tanh(0.5 * proj(1, D_HGRN))
    lf = jnp.log(f)
    k = 1.0 - f
    v_f32 = proj(2, D_HGRN)
    v = v_f32.astype(bf16)
    u = proj(4, D_CONV)

    row = lax.broadcasted_iota(jnp.int32, (BLOCK, BLOCK), 0)
    col = lax.broadcasted_iota(jnp.int32, (BLOCK, BLOCK), 1)
    causal = row >= col
    rel = (((col >= CHUNK) & (col <= row)).astype(f32)
           - ((col < CHUNK) & (col > row)).astype(f32)).astype(bf16)
    rel2 = jnp.concatenate([rel, rel], axis=1)
    n_blocks = n_rows // BLOCK
    rows = [slice(c * BLOCK, (c + 1) * BLOCK) for c in range(n_blocks)]
    heads = [slice(hd * HEAD_DIM, (hd + 1) * HEAD_DIM) for hd in range(N_HEADS)]

    lf_hi = lf.astype(bf16)
    lf_lo = (lf - lf_hi.astype(f32)).astype(bf16)
    b = jnp.concatenate(
        [jnp.dot(rel2, jnp.concatenate([lf_hi[r], lf_lo[r]], axis=0),
                 preferred_element_type=f32) for r in rows], axis=0)
    gate_c = proj(6, D_CONV)
    eb = jnp.exp(b)
    eib = jnp.exp(-b)
    q_dec = (q * eb).astype(bf16)
    k_inv = k * eib
    head = [eib[r][0:1, :] * f[r][0:1, :] for r in rows]
    tail = [eb[r][BLOCK - 1:BLOCK, :] for r in rows]
    k_end = [(k_inv[r] * tail[c]).astype(bf16) for c, r in enumerate(rows)]

    scores = [[jnp.dot(q_dec[r, sl], k_inv[r, sl].T.astype(bf16), preferred_element_type=f32)
               for sl in heads] for r in rows]
    upd_t = [[jnp.dot(v_f32[r, sl].T.astype(bf16), k_end[c][:, sl], preferred_element_type=f32)
              for sl in heads] for c, r in enumerate(rows)]
    gate_b = proj(5, D_CONV)
    state_before = [[None] * N_HEADS for _ in rows]
    for hd, sl in enumerate(heads):
        state_t = st_ref[hd]
        for c in range(n_blocks):
            state_before[c][hd] = (state_t * head[c][:, sl]).T.astype(bf16)
            state_t = state_t * (head[c][:, sl] * tail[c][:, sl]) + upd_t[c][hd]
        st_ref[hd] = state_t
    for c, r in enumerate(rows):
        for hd, sl in enumerate(heads):
            sc = jnp.where(causal, scores[c][hd], 0.0).astype(bf16)
            o_s[r0 + c * BLOCK:r0 + (c + 1) * BLOCK, sl] = (
                jnp.dot(sc, v[r, sl], preferred_element_type=f32)
                + jnp.dot(q_dec[r, sl], state_before[c][hd], preferred_element_type=f32))
    z_b = proj(7, D_CONV)
    z_a = proj(3, D_HGRN)

    hg = hg_ref[...]
    o_parts = []
    for sl in heads:
        oh = o_s[r0:r0 + n_rows, sl]
        oms = jnp.mean(oh * oh, axis=-1, keepdims=True)
        o_parts.append(oh * lax.rsqrt(oms + EPS) * hg[:, sl])
    o_a = (jnp.concatenate(o_parts, axis=-1) * _silu(z_a)).astype(bf16)

    cw = [cw_ref[j] for j in range(CONV_WIDTH)]
    cu = gate_c * u
    c0 = r0 + SUBLANES
    cu_ref[c0:c0 + n_rows, :] = cu
    conv = (cw[2] * cu
            + cw[1] * cu_ref[c0 - 1:c0 - 1 + n_rows, :]
            + cw[0] * cu_ref[c0 - 2:c0 - 2 + n_rows, :])
    y = gate_b * conv
    cg = cg_ref[...]
    lane = lax.broadcasted_iota(jnp.int32, (1, LANES), 1)
    first_group = lane < CONV_GROUP_DIM
    y_parts = []
    for j in range(D_CONV // LANES):
        sl = slice(j * LANES, (j + 1) * LANES)
        yj = y[:, sl]
        y2 = yj * yj
        tot = jnp.sum(y2, axis=-1, keepdims=True)
        lo = jnp.sum(jnp.where(first_group, y2, 0.0), axis=-1, keepdims=True)
        yms = jnp.where(first_group, lo, tot - lo) * (1.0 / CONV_GROUP_DIM)
        y_parts.append(yj * lax.rsqrt(yms + EPS) * cg[:, sl])
    o_b = (jnp.concatenate(y_parts, axis=-1) * _silu(z_b)).astype(bf16)

    if loader is not None:
        loader.need_all()
    mix = (jnp.dot(o_a, wout_ref[0:D_HGRN, :], preferred_element_type=f32)
           + jnp.dot(o_b, wout_ref[D_HGRN:D_HGRN + D_CONV, :], preferred_element_type=f32))
    res = x + mix
    rms = jnp.mean(res * res, axis=-1, keepdims=True)
    return res * lax.rsqrt(rms + EPS) * fg_ref[...]


W_IN_USE_ORDER = (0, 1, 2, 4, 6, 5, 7, 3)


class _WeightLoader:
    def __init__(self, win_hbm, wout_hbm, win_ref, wout_ref, stage, sems):
        assert win_ref.shape[1] == len(W_IN_USE_ORDER) * WEIGHT_SLAB
        cols = lambda j: pl.ds(j * WEIGHT_SLAB, WEIGHT_SLAB)
        self.slabs = [(win_hbm.at[0, :, cols(j)], win_ref.at[:, cols(j)]) for j in W_IN_USE_ORDER]
        self.slabs += [(wout_hbm.at[0, :, cols(j)], wout_ref.at[:, cols(j)])
                       for j in range(wout_ref.shape[1] // WEIGHT_SLAB)]
        self.stage, self.sems = stage, sems
        self.n_ready = 0
        for i in range(min(2, len(self.slabs))):
            self._copy(i).start()

    def _copy(self, i):
        return pltpu.make_async_copy(self.slabs[i][0], self.stage.at[i % 2], self.sems.at[i % 2])

    def _need(self, n):
        while self.n_ready < n:
            i = self.n_ready
            self._copy(i).wait()
            self.slabs[i][1][...] = self.stage[i % 2].astype(jnp.bfloat16)
            if i + 2 < len(self.slabs):
                self._copy(i + 2).start()
            self.n_ready += 1

    def need_w_in(self, j):
        self._need(W_IN_USE_ORDER.index(j) + 1)

    def need_all(self):
        self._need(len(self.slabs))


def _block_kernel(x_ref, g_ref, win_hbm, lbl_ref, cw_ref, hg_ref, cg_ref, wout_hbm, fg_ref,
                  out_ref, st_ref, cu_ref, o_s, win_ref, wout_ref, stage, sems):
    tile = x_ref.shape[0]
    sub = min(SUB_TILE, tile)

    @pl.when(pl.program_id(1) == 0)
    def _reset_carries():
        st_ref[...] = jnp.zeros_like(st_ref)
        cu_ref[0:SUBLANES, :] = jnp.zeros((SUBLANES, D_CONV), jnp.float32)

    def run_step(loader):
        lbl = lbl_ref[...]
        lmax = jnp.max(lbl, axis=0, keepdims=True)
        lexp = jnp.exp(lbl - lmax)
        lb = lexp[0:1, :] / jnp.sum(lexp, axis=0, keepdims=True)
        for r0 in range(0, tile, sub):
            out_ref[r0:r0 + sub, :] = _slab(
                x_ref[r0:r0 + sub, :], r0, lb, g_ref, win_ref, cw_ref, hg_ref, cg_ref, wout_ref,
                fg_ref, st_ref, cu_ref, o_s, loader if r0 == 0 else None)
        cu_ref[0:SUBLANES, :] = cu_ref[tile:tile + SUBLANES, :]

    first_step = (pl.program_id(0) == 0) & (pl.program_id(1) == 0)

    @pl.when(first_step)
    def _step_with_weight_load():
        run_step(_WeightLoader(win_hbm, wout_hbm, win_ref, wout_ref, stage, sems))

    @pl.when(jnp.logical_not(first_step))
    def _step():
        run_step(None)


@jax.jit
def kernel(x, norm_gain, w_in, lb_logits, conv_w, hgrn_norm_gain, conv_norm_gain, w_out,
           final_norm_gain):
    bsz, seq, d_model = x.shape
    depth = norm_gain.shape[0]
    assert depth == 1 and lb_logits.shape == (depth + 1, D_HGRN)
    assert w_in.shape == (depth, d_model, 4 * D_HGRN + 4 * D_CONV)
    assert w_out.shape == (depth, D_HGRN + D_CONV, d_model)
    tile = min(SEQ_TILE, seq)
    sub = min(SUB_TILE, tile)
    assert seq % tile == 0 and tile % sub == 0 and sub % BLOCK == 0
    in_cols = w_in.shape[2]
    assert in_cols % WEIGHT_SLAB == 0 and d_model % WEIGHT_SLAB == 0

    full = lambda shape: pl.BlockSpec(shape, lambda b, s: (0,) * len(shape))
    grid_spec = pltpu.PrefetchScalarGridSpec(
        num_scalar_prefetch=0,
        grid=(bsz, seq // tile),
        in_specs=[
            pl.BlockSpec((None, tile, d_model), lambda b, s: (b, s, 0)),
            full((1, d_model)),
            pl.BlockSpec(memory_space=pltpu.HBM),
            full((depth + 1, D_HGRN)),
            full((CONV_WIDTH, 1, D_CONV)),
            full((1, D_HGRN)),
            full((1, D_CONV)),
            pl.BlockSpec(memory_space=pltpu.HBM),
            full((1, d_model)),
        ],
        out_specs=pl.BlockSpec((None, tile, d_model), lambda b, s: (b, s, 0)),
        scratch_shapes=[
            pltpu.VMEM((N_HEADS, HEAD_DIM, HEAD_DIM), jnp.float32),
            pltpu.VMEM((tile + SUBLANES, D_CONV), jnp.float32),
            pltpu.VMEM((tile, D_HGRN), jnp.float32),
            pltpu.VMEM((d_model, in_cols), jnp.bfloat16),
            pltpu.VMEM((D_HGRN + D_CONV, d_model), jnp.bfloat16),
            pltpu.VMEM((2, d_model, WEIGHT_SLAB), jnp.float32),
            pltpu.SemaphoreType.DMA((2,)),
        ],
    )
    return pl.pallas_call(
        _block_kernel,
        grid_spec=grid_spec,
        out_shape=jax.ShapeDtypeStruct(x.shape, x.dtype),
        compiler_params=pltpu.CompilerParams(
            dimension_semantics=("arbitrary", "arbitrary"),
            vmem_limit_bytes=VMEM_LIMIT_BYTES),
        name="hgrn2_shortconv_block",
    )(x, norm_gain, w_in, lb_logits, jnp.transpose(conv_w, (1, 0, 2)), hgrn_norm_gain,
      conv_norm_gain, w_out, final_norm_gain.reshape(1, d_model))
```

```python
import jax
import jax.numpy as jnp
from jax import lax
from jax.experimental import pallas as pl
from jax.experimental.pallas import tpu as pltpu

D_HGRN = 512
D_CONV = 512
HEAD_DIM = 128
N_HEADS = D_HGRN // HEAD_DIM
CHUNK = 64
BLOCK = 2 * CHUNK
CONV_WIDTH = 3
CONV_GROUP_DIM = 64
EPS = 1e-6
SEQ_TILE = 1024
SUB_TILE = 1024
LANES = 128
SUBLANES = 8
VMEM_LIMIT_BYTES = 60000 * 1024
WEIGHT_SLAB = 512

_NT = (((1,), (1,)), ((), ()))
_TN = (((0,), (0,)), ((), ()))


def _silu(z):
    hz = 0.5 * z
    return hz + hz * jnp.tanh(hz)


def _slab(x, r0, lb, g_ref, win_ref, cw_ref, hg_ref, cg_ref, wout_ref, fg_ref,
          st_ref, cu_ref, o_s, loader=None):
    f32, bf16 = jnp.float32, jnp.bfloat16
    n_rows = x.shape[0]

    ms = jnp.mean(x * x, axis=-1, keepdims=True)
    h = (x * lax.rsqrt(ms + EPS)).astype(bf16)

    def proj(j, width):
        if loader is not None:
            loader.need_w_in(j)
        return jnp.dot(h, win_ref[:, j * width:(j + 1) * width], preferred_element_type=f32)

    q = proj(0, D_HGRN)
    f = (0.5 + 0.5 * lb) + (0.5 - 0.5 * lb) * jnp.tanh(0.5 * proj(1, D_HGRN))
    lf = jnp.log(f)
    k = 1.0 - f
    v = proj(2, D_HGRN).astype(bf16)
    u = proj(4, D_CONV)

    row = lax.broadcasted_iota(jnp.int32, (BLOCK, BLOCK), 0)
    col = lax.broadcasted_iota(jnp.int32, (BLOCK, BLOCK), 1)
    causal = row >= col
    rel = (((col >= CHUNK) & (col <= row)).astype(f32)
           - ((col < CHUNK) & (col > row)).astype(f32)).astype(bf16)
    rel2 = jnp.concatenate([rel, rel], axis=1)
    n_blocks = n_rows // BLOCK
    rows = [slice(c * BLOCK, (c + 1) * BLOCK) for c in range(n_blocks)]
    heads = [slice(hd * HEAD_DIM, (hd + 1) * HEAD_DIM) for hd in range(N_HEADS)]

    lf_hi = lf.astype(bf16)
    lf_lo = (lf - lf_hi.astype(f32)).astype(bf16)
    b = jnp.concatenate(
        [jnp.dot(rel2, jnp.concatenate([lf_hi[r], lf_lo[r]], axis=0),
                 preferred_element_type=f32) for r in rows], axis=0)
    gate_c = proj(6, D_CONV)
    eb = jnp.exp(b)
    eib = jnp.exp(-b)
    q_dec = (q * eb).astype(bf16)
    k_inv = k * eib
    head = [eib[r][0:1, :] * f[r][0:1, :] for r in rows]
    tail = [eb[r][BLOCK - 1:BLOCK, :] for r in rows]
    k_end = [(k_inv[r] * tail[c]).astype(bf16) for c, r in enumerate(rows)]

    scores = [[jnp.dot(q_dec[r, sl], k_inv[r, sl].T.astype(bf16), preferred_element_type=f32)
               for sl in heads] for r in rows]
    upd_t = [[lax.dot_general(v[r, sl], k_end[c][:, sl], _TN, preferred_element_type=f32)
              for sl in heads] for c, r in enumerate(rows)]
    gate_b = proj(5, D_CONV)
    state_before = [[None] * N_HEADS for _ in rows]
    for hd, sl in enumerate(heads):
        state_t = st_ref[hd]
        for c in range(n_blocks):
            state_before[c][hd] = (state_t * head[c][:, sl]).T.astype(bf16)
            state_t = state_t * (head[c][:, sl] * tail[c][:, sl]) + upd_t[c][hd]
        st_ref[hd] = state_t
    for c, r in enumerate(rows):
        for hd, sl in enumerate(heads):
            sc = jnp.where(causal, scores[c][hd], 0.0).astype(bf16)
            o_s[r0 + c * BLOCK:r0 + (c + 1) * BLOCK, sl] = (
                jnp.dot(sc, v[r, sl], preferred_element_type=f32)
                + jnp.dot(q_dec[r, sl], state_before[c][hd], preferred_element_type=f32))
    z_b = proj(7, D_CONV)
    z_a = proj(3, D_HGRN)

    o_parts = []
    for sl in heads:
        oh = o_s[r0:r0 + n_rows, sl]
        oms = jnp.mean(oh * oh, axis=-1, keepdims=True)
        o_parts.append(oh * lax.rsqrt(oms + EPS))
    o_a = (jnp.concatenate(o_parts, axis=-1) * _silu(z_a)).astype(bf16)

    cw = [cw_ref[j] for j in range(CONV_WIDTH)]
    cu = gate_c * u
    c0 = r0 + SUBLANES
    cu_ref[c0:c0 + n_rows, :] = cu
    conv = (cw[2] * cu
            + cw[1] * cu_ref[c0 - 1:c0 - 1 + n_rows, :]
            + cw[0] * cu_ref[c0 - 2:c0 - 2 + n_rows, :])
    y = gate_b * conv
    lane = lax.broadcasted_iota(jnp.int32, (1, LANES), 1)
    first_group = lane < CONV_GROUP_DIM
    y_parts = []
    for j in range(D_CONV // LANES):
        sl = slice(j * LANES, (j + 1) * LANES)
        yj = y[:, sl]
        y2 = yj * yj
        tot = jnp.sum(y2, axis=-1, keepdims=True)
        lo = jnp.sum(jnp.where(first_group, y2, 0.0), axis=-1, keepdims=True)
        yms = jnp.where(first_group, lo, tot - lo) * (1.0 / CONV_GROUP_DIM)
        y_parts.append(yj * lax.rsqrt(yms + EPS))
    o_b = (jnp.concatenate(y_parts, axis=-1) * _silu(z_b)).astype(bf16)

    if loader is not None:
        loader.need_all()
    mix = (jnp.dot(o_a, wout_ref[0:D_HGRN, :], preferred_element_type=f32)
           + jnp.dot(o_b, wout_ref[D_HGRN:D_HGRN + D_CONV, :], preferred_element_type=f32))
    res = x + mix
    rms = jnp.mean(res * res, axis=-1, keepdims=True)
    return res * lax.rsqrt(rms + EPS) * fg_ref[...]


W_IN_USE_ORDER = (0, 1, 2, 4, 6, 5, 7, 3)


class _WeightLoader:
    def __init__(self, win_hbm, wout_hbm, win_ref, wout_ref, stage, sems, in_gain, mix_gain):
        assert win_ref.shape[1] == len(W_IN_USE_ORDER) * WEIGHT_SLAB
        cols = lambda j: pl.ds(j * WEIGHT_SLAB, WEIGHT_SLAB)
        self.slabs = [(win_hbm.at[0, :, cols(j)], win_ref.at[:, cols(j)], in_gain)
                      for j in W_IN_USE_ORDER]
        self.slabs += [(wout_hbm.at[0, :, cols(j)], wout_ref.at[:, cols(j)], mix_gain)
                       for j in range(wout_ref.shape[1] // WEIGHT_SLAB)]
        self.stage, self.sems = stage, sems
        self.n_ready = 0
        for i in range(min(2, len(self.slabs))):
            self._copy(i).start()

    def _copy(self, i):
        return pltpu.make_async_copy(self.slabs[i][0], self.stage.at[i % 2], self.sems.at[i % 2])

    def _need(self, n):
        while self.n_ready < n:
            i = self.n_ready
            self._copy(i).wait()
            self.slabs[i][1][...] = (self.stage[i % 2] * self.slabs[i][2]).astype(jnp.bfloat16)
            if i + 2 < len(self.slabs):
                self._copy(i + 2).start()
            self.n_ready += 1

    def need_w_in(self, j):
        self._need(W_IN_USE_ORDER.index(j) + 1)

    def need_all(self):
        self._need(len(self.slabs))


def _block_kernel(x_ref, g_ref, win_hbm, lbl_ref, cw_ref, hg_ref, cg_ref, wout_hbm, fg_ref,
                  out_ref, st_ref, cu_ref, o_s, win_ref, wout_ref, stage, sems):
    tile = x_ref.shape[0]
    sub = min(SUB_TILE, tile)

    @pl.when(pl.program_id(1) == 0)
    def _reset_carries():
        st_ref[...] = jnp.zeros_like(st_ref)
        cu_ref[0:SUBLANES, :] = jnp.zeros((SUBLANES, D_CONV), jnp.float32)

    def run_step(loader):
        lbl = lbl_ref[...]
        lmax = jnp.max(lbl, axis=0, keepdims=True)
        lexp = jnp.exp(lbl - lmax)
        lb = lexp[0:1, :] / jnp.sum(lexp, axis=0, keepdims=True)
        for r0 in range(0, tile, sub):
            out_ref[r0:r0 + sub, :] = _slab(
                x_ref[r0:r0 + sub, :], r0, lb, g_ref, win_ref, cw_ref, hg_ref, cg_ref, wout_ref,
                fg_ref, st_ref, cu_ref, o_s, loader if r0 == 0 else None)
        cu_ref[0:SUBLANES, :] = cu_ref[tile:tile + SUBLANES, :]

    first_step = (pl.program_id(0) == 0) & (pl.program_id(1) == 0)

    @pl.when(first_step)
    def _step_with_weight_load():
        in_gain = jnp.transpose(g_ref[...])
        mix_gain = jnp.transpose(jnp.concatenate([hg_ref[...], cg_ref[...]], axis=1))
        run_step(_WeightLoader(win_hbm, wout_hbm, win_ref, wout_ref, stage, sems,
                               in_gain, mix_gain))

    @pl.when(jnp.logical_not(first_step))
    def _step():
        run_step(None)


@jax.jit
def kernel(x, norm_gain, w_in, lb_logits, conv_w, hgrn_norm_gain, conv_norm_gain, w_out,
           final_norm_gain):
    bsz, seq, d_model = x.shape
    depth = norm_gain.shape[0]
    assert depth == 1 and lb_logits.shape == (depth + 1, D_HGRN)
    assert w_in.shape == (depth, d_model, 4 * D_HGRN + 4 * D_CONV)
    assert w_out.shape == (depth, D_HGRN + D_CONV, d_model)
    tile = min(SEQ_TILE, seq)
    sub = min(SUB_TILE, tile)
    assert seq % tile == 0 and tile % sub == 0 and sub % BLOCK == 0
    in_cols = w_in.shape[2]
    assert in_cols % WEIGHT_SLAB == 0 and d_model % WEIGHT_SLAB == 0

    full = lambda shape: pl.BlockSpec(shape, lambda b, s: (0,) * len(shape))
    grid_spec = pltpu.PrefetchScalarGridSpec(
        num_scalar_prefetch=0,
        grid=(bsz, seq // tile),
        in_specs=[
            pl.BlockSpec((None, tile, d_model), lambda b, s: (b, s, 0)),
            full((1, d_model)),
            pl.BlockSpec(memory_space=pltpu.HBM),
            full((depth + 1, D_HGRN)),
            full((CONV_WIDTH, 1, D_CONV)),
            full((1, D_HGRN)),
            full((1, D_CONV)),
            pl.BlockSpec(memory_space=pltpu.HBM),
            full((1, d_model)),
        ],
        out_specs=pl.BlockSpec((None, tile, d_model), lambda b, s: (b, s, 0)),
        scratch_shapes=[
            pltpu.VMEM((N_HEADS, HEAD_DIM, HEAD_DIM), jnp.float32),
            pltpu.VMEM((tile + SUBLANES, D_CONV), jnp.float32),
            pltpu.VMEM((tile, D_HGRN), jnp.float32),
            pltpu.VMEM((d_model, in_cols), jnp.bfloat16),
            pltpu.VMEM((D_HGRN + D_CONV, d_model), jnp.bfloat16),
            pltpu.VMEM((2, d_model, WEIGHT_SLAB), jnp.float32),
            pltpu.SemaphoreType.DMA((2,)),
        ],
    )
    return pl.pallas_call(
        _block_kernel,
        grid_spec=grid_spec,
        out_shape=jax.ShapeDtypeStruct(x.shape, x.dtype),
        compiler_params=pltpu.CompilerParams(
            dimension_semantics=("arbitrary", "arbitrary"),
            vmem_limit_bytes=VMEM_LIMIT_BYTES),
        name="hgrn2_shortconv_block",
    )(x, norm_gain, w_in, lb_logits, jnp.transpose(conv_w, (1, 0, 2)), hgrn_norm_gain,
      conv_norm_gain, w_out, final_norm_gain.reshape(1, d_model))
```

```python
import jax
import jax.numpy as jnp
from jax import lax
from jax.experimental import pallas as pl
from jax.experimental.pallas import tpu as pltpu

D_HGRN = 512
D_CONV = 512
HEAD_DIM = 128
N_HEADS = D_HGRN // HEAD_DIM
CHUNK = 64
BLOCK = 2 * CHUNK
CONV_WIDTH = 3
CONV_GROUP_DIM = 64
EPS = 1e-6
SEQ_TILE = 1024
SUB_TILE = 1024
LANES = 128
SUBLANES = 8
VMEM_LIMIT_BYTES = 60000 * 1024
WEIGHT_SLAB = 512

_NT = (((1,), (1,)), ((), ()))
_TN = (((0,), (0,)), ((), ()))


def _silu(z):
    hz = 0.5 * z
    return hz + hz * jnp.tanh(hz)


def _slab(x, r0, lb, g_ref, win_ref, cw_ref, hg_ref, cg_ref, wout_ref, fg_ref,
          st_ref, cu_ref, o_s, loader=None):
    f32, bf16 = jnp.float32, jnp.bfloat16
    n_rows = x.shape[0]

    ms = jnp.mean(x * x, axis=-1, keepdims=True)
    h = (x * lax.rsqrt(ms + EPS) * g_ref[...]).astype(bf16)

    def proj(j, width):
        if loader is not None:
            loader.need_w_in(j)
        return jnp.dot(h, win_ref[:, j * width:(j + 1) * width], preferred_element_type=f32)

    q = proj(0, D_HGRN)
    f = (0.5 + 0.5 * lb) + (0.5 - 0.5 * lb) * jnp.tanh(0.5 * proj(1, D_HGRN))
    lf = jnp.log(f)
    k = 1.0 - f
    v = proj(2, D_HGRN).astype(bf16)
    u = proj(4, D_CONV)

    row = lax.broadcasted_iota(jnp.int32, (BLOCK, BLOCK), 0)
    col = lax.broadcasted_iota(jnp.int32, (BLOCK, BLOCK), 1)
    causal = row >= col
    rel = (((col >= CHUNK) & (col <= row)).astype(f32)
           - ((col < CHUNK) & (col > row)).astype(f32)).astype(bf16)
    rel2 = jnp.concatenate([rel, rel], axis=1)
    n_blocks = n_rows // BLOCK
    rows = [slice(c * BLOCK, (c + 1) * BLOCK) for c in range(n_blocks)]
    heads = [slice(hd * HEAD_DIM, (hd + 1) * HEAD_DIM) for hd in range(N_HEADS)]

    lf_hi = lf.astype(bf16)
    lf_lo = (lf - lf_hi.astype(f32)).astype(bf16)
    b = jnp.concatenate(
        [jnp.dot(rel2, jnp.concatenate([lf_hi[r], lf_lo[r]], axis=0),
                 preferred_element_type=f32) for r in rows], axis=0)
    gate_c = proj(6, D_CONV)
    eb = jnp.exp(b)
    eib = jnp.exp(-b)
    q_dec = (q * eb).astype(bf16)
    k_inv = k * eib
    head = [eib[r][0:1, :] * f[r][0:1, :] for r in rows]
    tail = [eb[r][BLOCK - 1:BLOCK, :] for r in rows]
    k_end = [(k_inv[r] * tail[c]).astype(bf16) for c, r in enumerate(rows)]

    scores = [[jnp.dot(q_dec[r, sl], k_inv[r, sl].T.astype(bf16), preferred_element_type=f32)
               for sl in heads] for r in rows]
    upd_t = [[lax.dot_general(v[r, sl], k_end[c][:, sl], _TN, preferred_element_type=f32)
              for sl in heads] for c, r in enumerate(rows)]
    gate_b = proj(5, D_CONV)
    state_before = [[None] * N_HEADS for _ in rows]
    for hd, sl in enumerate(heads):
        state_t = st_ref[hd]
        for c in range(n_blocks):
            state_before[c][hd] = (state_t * head[c][:, sl]).T.astype(bf16)
            state_t = state_t * (head[c][:, sl] * tail[c][:, sl]) + upd_t[c][hd]
        st_ref[hd] = state_t
    for c, r in enumerate(rows):
        for hd, sl in enumerate(heads):
            sc = jnp.where(causal, scores[c][hd], 0.0).astype(bf16)
            o_s[r0 + c * BLOCK:r0 + (c + 1) * BLOCK, sl] = (
                jnp.dot(sc, v[r, sl], preferred_element_type=f32)
                + jnp.dot(q_dec[r, sl], state_before[c][hd], preferred_element_type=f32))
    z_b = proj(7, D_CONV)
    z_a = proj(3, D_HGRN)

    hg = hg_ref[...]
    o_parts = []
    for sl in heads:
        oh = o_s[r0:r0 + n_rows, sl]
        oms = jnp.mean(oh * oh, axis=-1, keepdims=True)
        o_parts.append(oh * lax.rsqrt(oms + EPS) * hg[:, sl])
    o_a = (jnp.concatenate(o_parts, axis=-1) * _silu(z_a)).astype(bf16)

    cw = [cw_ref[j] for j in range(CONV_WIDTH)]
    cu = gate_c * u
    c0 = r0 + SUBLANES
    cu_ref[c0:c0 + n_rows, :] = cu
    conv = (cw[2] * cu
            + cw[1] * cu_ref[c0 - 1:c0 - 1 + n_rows, :]
            + cw[0] * cu_ref[c0 - 2:c0 - 2 + n_rows, :])
    y = gate_b * conv
    cg = cg_ref[...]
    lane = lax.broadcasted_iota(jnp.int32, (1, LANES), 1)
    first_group = lane < CONV_GROUP_DIM
    y_parts = []
    for j in range(D_CONV // LANES):
        sl = slice(j * LANES, (j + 1) * LANES)
        yj = y[:, sl]
        y2 = yj * yj
        tot = jnp.sum(y2, axis=-1, keepdims=True)
        lo = jnp.sum(jnp.where(first_group, y2, 0.0), axis=-1, keepdims=True)
        yms = jnp.where(first_group, lo, tot - lo) * (1.0 / CONV_GROUP_DIM)
        y_parts.append(yj * lax.rsqrt(yms + EPS) * cg[:, sl])
    o_b = (jnp.concatenate(y_parts, axis=-1) * _silu(z_b)).astype(bf16)

    if loader is not None:
        loader.need_all()
    mix = jnp.dot(jnp.concatenate([o_a, o_b], axis=1), wout_ref[...],
                  preferred_element_type=f32)
    res = x + mix
    rms = jnp.mean(res * res, axis=-1, keepdims=True)
    return res * lax.rsqrt(rms + EPS) * fg_ref[...]


W_IN_USE_ORDER = (0, 1, 2, 4, 6, 5, 7, 3)


class _WeightLoader:
    def __init__(self, win_hbm, wout_hbm, win_ref, wout_ref, stage, sems):
        assert win_ref.shape[1] == len(W_IN_USE_ORDER) * WEIGHT_SLAB
        cols = lambda j: pl.ds(j * WEIGHT_SLAB, WEIGHT_SLAB)
        self.slabs = [(win_hbm.at[0, :, cols(j)], win_ref.at[:, cols(j)]) for j in W_IN_USE_ORDER]
        self.slabs += [(wout_hbm.at[0, :, cols(j)], wout_ref.at[:, cols(j)])
                       for j in range(wout_ref.shape[1] // WEIGHT_SLAB)]
        self.stage, self.sems = stage, sems
        self.n_ready = 0
        for i in range(min(2, len(self.slabs))):
            self._copy(i).start()

    def _copy(self, i):
        return pltpu.make_async_copy(self.slabs[i][0], self.stage.at[i % 2], self.sems.at[i % 2])

    def _need(self, n):
        while self.n_ready < n:
            i = self.n_ready
            self._copy(i).wait()
            self.slabs[i][1][...] = self.stage[i % 2].astype(jnp.bfloat16)
            if i + 2 < len(self.slabs):
                self._copy(i + 2).start()
            self.n_ready += 1

    def need_w_in(self, j):
        self._need(W_IN_USE_ORDER.index(j) + 1)

    def need_all(self):
        self._need(len(self.slabs))


def _block_kernel(x_ref, g_ref, win_hbm, lbl_ref, cw_ref, hg_ref, cg_ref, wout_hbm, fg_ref,
                  out_ref, st_ref, cu_ref, o_s, win_ref, wout_ref, stage, sems):
    tile = x_ref.shape[0]
    sub = min(SUB_TILE, tile)

    @pl.when(pl.program_id(1) == 0)
    def _reset_carries():
        st_ref[...] = jnp.zeros_like(st_ref)
        cu_ref[0:SUBLANES, :] = jnp.zeros((SUBLANES, D_CONV), jnp.float32)

    def run_step(loader):
        lbl = lbl_ref[...]
        lmax = jnp.max(lbl, axis=0, keepdims=True)
        lexp = jnp.exp(lbl - lmax)
        lb = lexp[0:1, :] / jnp.sum(lexp, axis=0, keepdims=True)
        for r0 in range(0, tile, sub):
            out_ref[r0:r0 + sub, :] = _slab(
                x_ref[r0:r0 + sub, :], r0, lb, g_ref, win_ref, cw_ref, hg_ref, cg_ref, wout_ref,
                fg_ref, st_ref, cu_ref, o_s, loader if r0 == 0 else None)
        cu_ref[0:SUBLANES, :] = cu_ref[tile:tile + SUBLANES, :]

    first_step = (pl.program_id(0) == 0) & (pl.program_id(1) == 0)

    @pl.when(first_step)
    def _step_with_weight_load():
        run_step(_WeightLoader(win_hbm, wout_hbm, win_ref, wout_ref, stage, sems))

    @pl.when(jnp.logical_not(first_step))
    def _step():
        run_step(None)


@jax.jit
def kernel(x, norm_gain, w_in, lb_logits, conv_w, hgrn_norm_gain, conv_norm_gain, w_out,
           final_norm_gain):
    bsz, seq, d_model = x.shape
    depth = norm_gain.shape[0]
    assert depth == 1 and lb_logits.shape == (depth + 1, D_HGRN)
    assert w_in.shape == (depth, d_model, 4 * D_HGRN + 4 * D_CONV)
    assert w_out.shape == (depth, D_HGRN + D_CONV, d_model)
    tile = min(SEQ_TILE, seq)
    sub = min(SUB_TILE, tile)
    assert seq % tile == 0 and tile % sub == 0 and sub % BLOCK == 0
    in_cols = w_in.shape[2]
    assert in_cols % WEIGHT_SLAB == 0 and d_model % WEIGHT_SLAB == 0

    full = lambda shape: pl.BlockSpec(shape, lambda b, s: (0,) * len(shape))
    grid_spec = pltpu.PrefetchScalarGridSpec(
        num_scalar_prefetch=0,
        grid=(bsz, seq // tile),
        in_specs=[
            pl.BlockSpec((None, tile, d_model), lambda b, s: (b, s, 0)),
            full((1, d_model)),
            pl.BlockSpec(memory_space=pltpu.HBM),
            full((depth + 1, D_HGRN)),
            full((CONV_WIDTH, 1, D_CONV)),
            full((1, D_HGRN)),
            full((1, D_CONV)),
            pl.BlockSpec(memory_space=pltpu.HBM),
            full((1, d_model)),
        ],
        out_specs=pl.BlockSpec((None, tile, d_model), lambda b, s: (b, s, 0)),
        scratch_shapes=[
            pltpu.VMEM((N_HEADS, HEAD_DIM, HEAD_DIM), jnp.float32),
            pltpu.VMEM((tile + SUBLANES, D_CONV), jnp.float32),
            pltpu.VMEM((tile, D_HGRN), jnp.float32),
            pltpu.VMEM((d_model, in_cols), jnp.bfloat16),
            pltpu.VMEM((D_HGRN + D_CONV, d_model), jnp.bfloat16),
            pltpu.VMEM((2, d_model, WEIGHT_SLAB), jnp.float32),
            pltpu.SemaphoreType.DMA((2,)),
        ],
    )
    return pl.pallas_call(
        _block_kernel,
        grid_spec=grid_spec,
        out_shape=jax.ShapeDtypeStruct(x.shape, x.dtype),
        compiler_params=pltpu.CompilerParams(
            dimension_semantics=("arbitrary", "arbitrary"),
            vmem_limit_bytes=VMEM_LIMIT_BYTES),
        name="hgrn2_shortconv_block",
    )(x, norm_gain, w_in, lb_logits, jnp.transpose(conv_w, (1, 0, 2)), hgrn_norm_gain,
      conv_norm_gain, w_out, final_norm_gain.reshape(1, d_model))
```

```python
import jax
import jax.numpy as jnp
from jax import lax
from jax.experimental import pallas as pl
from jax.experimental.pallas import tpu as pltpu

D_HGRN = 512
D_CONV = 512
HEAD_DIM = 128
N_HEADS = D_HGRN // HEAD_DIM
CHUNK = 64
BLOCK = 2 * CHUNK
CONV_WIDTH = 3
CONV_GROUP_DIM = 64
EPS = 1e-6
SEQ_TILE = 1024
SUB_TILE = 1024
LANES = 128
SUBLANES = 8
VMEM_LIMIT_BYTES = 60000 * 1024
WEIGHT_SLAB = 512

_NT = (((1,), (1,)), ((), ()))
_TN = (((0,), (0,)), ((), ()))


def _silu(z):
    hz = 0.5 * z
    return hz + hz * jnp.tanh(hz)


def _slab(x, r0, lb, g_ref, win_ref, cw_ref, hg_ref, cg_ref, wout_ref, fg_ref,
          st_ref, cu_ref, o_s, loader=None):
    f32, bf16 = jnp.float32, jnp.bfloat16
    n_rows = x.shape[0]

    ms = jnp.mean(x * x, axis=-1, keepdims=True)
    h = (x * lax.rsqrt(ms + EPS) * g_ref[...]).astype(bf16)

    def proj(j, width):
        if loader is not None:
            loader.need_w_in(j)
        return jnp.dot(h, win_ref[:, j * width:(j + 1) * width], preferred_element_type=f32)

    if loader is not None:
        loader.need_w_in(2)
    qfv = jnp.dot(h, win_ref[:, 0:3 * D_HGRN], preferred_element_type=f32)
    q = qfv[:, 0:D_HGRN]
    f = (0.5 + 0.5 * lb) + (0.5 - 0.5 * lb) * jnp.tanh(0.5 * qfv[:, D_HGRN:2 * D_HGRN])
    lf = jnp.log(f)
    k = 1.0 - f
    v = qfv[:, 2 * D_HGRN:3 * D_HGRN].astype(bf16)
    u = proj(4, D_CONV)

    row = lax.broadcasted_iota(jnp.int32, (BLOCK, BLOCK), 0)
    col = lax.broadcasted_iota(jnp.int32, (BLOCK, BLOCK), 1)
    causal = row >= col
    rel = (((col >= CHUNK) & (col <= row)).astype(f32)
           - ((col < CHUNK) & (col > row)).astype(f32)).astype(bf16)
    rel2 = jnp.concatenate([rel, rel], axis=1)
    n_blocks = n_rows // BLOCK
    rows = [slice(c * BLOCK, (c + 1) * BLOCK) for c in range(n_blocks)]
    heads = [slice(hd * HEAD_DIM, (hd + 1) * HEAD_DIM) for hd in range(N_HEADS)]

    lf_hi = lf.astype(bf16)
    lf_lo = (lf - lf_hi.astype(f32)).astype(bf16)
    b = jnp.concatenate(
        [jnp.dot(rel2, jnp.concatenate([lf_hi[r], lf_lo[r]], axis=0),
                 preferred_element_type=f32) for r in rows], axis=0)
    gate_c = proj(6, D_CONV)
    eb = jnp.exp(b)
    eib = jnp.exp(-b)
    q_dec = (q * eb).astype(bf16)
    k_inv = k * eib
    head = [eib[r][0:1, :] * f[r][0:1, :] for r in rows]
    tail = [eb[r][BLOCK - 1:BLOCK, :] for r in rows]
    k_end = [(k_inv[r] * tail[c]).astype(bf16) for c, r in enumerate(rows)]

    scores = [[jnp.dot(q_dec[r, sl], k_inv[r, sl].T.astype(bf16), preferred_element_type=f32)
               for sl in heads] for r in rows]
    upd_t = [[lax.dot_general(v[r, sl], k_end[c][:, sl], _TN, preferred_element_type=f32)
              for sl in heads] for c, r in enumerate(rows)]
    gate_b = proj(5, D_CONV)
    state_before = [[None] * N_HEADS for _ in rows]
    for hd, sl in enumerate(heads):
        state_t = st_ref[hd]
        for c in range(n_blocks):
            state_before[c][hd] = (state_t * head[c][:, sl]).T.astype(bf16)
            state_t = state_t * (head[c][:, sl] * tail[c][:, sl]) + upd_t[c][hd]
        st_ref[hd] = state_t
    for c, r in enumerate(rows):
        for hd, sl in enumerate(heads):
            sc = jnp.where(causal, scores[c][hd], 0.0).astype(bf16)
            o_s[r0 + c * BLOCK:r0 + (c + 1) * BLOCK, sl] = (
                jnp.dot(sc, v[r, sl], preferred_element_type=f32)
                + jnp.dot(q_dec[r, sl], state_before[c][hd], preferred_element_type=f32))
    z_b = proj(7, D_CONV)
    z_a = proj(3, D_HGRN)

    hg = hg_ref[...]
    o_parts = []
    for sl in heads:
        oh = o_s[r0:r0 + n_rows, sl]
        oms = jnp.mean(oh * oh, axis=-1, keepdims=True)
        o_parts.append(oh * lax.rsqrt(oms + EPS) * hg[:, sl])
    o_a = (jnp.concatenate(o_parts, axis=-1) * _silu(z_a)).astype(bf16)

    cw = [cw_ref[j] for j in range(CONV_WIDTH)]
    cu = gate_c * u
    c0 = r0 + SUBLANES
    cu_ref[c0:c0 + n_rows, :] = cu
    conv = (cw[2] * cu
            + cw[1] * cu_ref[c0 - 1:c0 - 1 + n_rows, :]
            + cw[0] * cu_ref[c0 - 2:c0 - 2 + n_rows, :])
    y = gate_b * conv
    cg = cg_ref[...]
    lane = lax.broadcasted_iota(jnp.int32, (1, LANES), 1)
    first_group = lane < CONV_GROUP_DIM
    y_parts = []
    for j in range(D_CONV // LANES):
        sl = slice(j * LANES, (j + 1) * LANES)
        yj = y[:, sl]
        y2 = yj * yj
        tot = jnp.sum(y2, axis=-1, keepdims=True)
        lo = jnp.sum(jnp.where(first_group, y2, 0.0), axis=-1, keepdims=True)
        yms = jnp.where(first_group, lo, tot - lo) * (1.0 / CONV_GROUP_DIM)
        y_parts.append(yj * lax.rsqrt(yms + EPS) * cg[:, sl])
    o_b = (jnp.concatenate(y_parts, axis=-1) * _silu(z_b)).astype(bf16)

    if loader is not None:
        loader.need_all()
    mix = (jnp.dot(o_a, wout_ref[0:D_HGRN, :], preferred_element_type=f32)
           + jnp.dot(o_b, wout_ref[D_HGRN:D_HGRN + D_CONV, :], preferred_element_type=f32))
    res = x + mix
    rms = jnp.mean(res * res, axis=-1, keepdims=True)
    return res * lax.rsqrt(rms + EPS) * fg_ref[...]


W_IN_USE_ORDER = (0, 1, 2, 4, 6, 5, 7, 3)


class _WeightLoader:
    def __init__(self, win_hbm, wout_hbm, win_ref, wout_ref, stage, sems):
        assert win_ref.shape[1] == len(W_IN_USE_ORDER) * WEIGHT_SLAB
        cols = lambda j: pl.ds(j * WEIGHT_SLAB, WEIGHT_SLAB)
        self.slabs = [(win_hbm.at[0, :, cols(j)], win_ref.at[:, cols(j)]) for j in W_IN_USE_ORDER]
        self.slabs += [(wout_hbm.at[0, :, cols(j)], wout_ref.at[:, cols(j)])
                       for j in range(wout_ref.shape[1] // WEIGHT_SLAB)]
        self.stage, self.sems = stage, sems
        self.n_ready = 0
        for i in range(min(2, len(self.slabs))):
            self._copy(i).start()

    def _copy(self, i):
        return pltpu.make_async_copy(self.slabs[i][0], self.stage.at[i % 2], self.sems.at[i % 2])

    def _need(self, n):
        while self.n_ready < n:
            i = self.n_ready
            self._copy(i).wait()
            self.slabs[i][1][...] = self.stage[i % 2].astype(jnp.bfloat16)
            if i + 2 < len(self.slabs):
                self._copy(i + 2).start()
            self.n_ready += 1

    def need_w_in(self, j):
        self._need(W_IN_USE_ORDER.index(j) + 1)

    def need_all(self):
        self._need(len(self.slabs))


def _block_kernel(x_ref, g_ref, win_hbm, lbl_ref, cw_ref, hg_ref, cg_ref, wout_hbm, fg_ref,
                  out_ref, st_ref, cu_ref, o_s, win_ref, wout_ref, stage, sems):
    tile = x_ref.shape[0]
    sub = min(SUB_TILE, tile)

    @pl.when(pl.program_id(1) == 0)
    def _reset_carries():
        st_ref[...] = jnp.zeros_like(st_ref)
        cu_ref[0:SUBLANES, :] = jnp.zeros((SUBLANES, D_CONV), jnp.float32)

    def run_step(loader):
        lbl = lbl_ref[...]
        lmax = jnp.max(lbl, axis=0, keepdims=True)
        lexp = jnp.exp(lbl - lmax)
        lb = lexp[0:1, :] / jnp.sum(lexp, axis=0, keepdims=True)
        for r0 in range(0, tile, sub):
            out_ref[r0:r0 + sub, :] = _slab(
                x_ref[r0:r0 + sub, :], r0, lb, g_ref, win_ref, cw_ref, hg_ref, cg_ref, wout_ref,
                fg_ref, st_ref, cu_ref, o_s, loader if r0 == 0 else None)
        cu_ref[0:SUBLANES, :] = cu_ref[tile:tile + SUBLANES, :]

    first_step = (pl.program_id(0) == 0) & (pl.program_id(1) == 0)

    @pl.when(first_step)
    def _step_with_weight_load():
        run_step(_WeightLoader(win_hbm, wout_hbm, win_ref, wout_ref, stage, sems))

    @pl.when(jnp.logical_not(first_step))
    def _step():
        run_step(None)


@jax.jit
def kernel(x, norm_gain, w_in, lb_logits, conv_w, hgrn_norm_gain, conv_norm_gain, w_out,
           final_norm_gain):
    bsz, seq, d_model = x.shape
    depth = norm_gain.shape[0]
    assert depth == 1 and lb_logits.shape == (depth + 1, D_HGRN)
    assert w_in.shape == (depth, d_model, 4 * D_HGRN + 4 * D_CONV)
    assert w_out.shape == (depth, D_HGRN + D_CONV, d_model)
    tile = min(SEQ_TILE, seq)
    sub = min(SUB_TILE, tile)
    assert seq % tile == 0 and tile % sub == 0 and sub % BLOCK == 0
    in_cols = w_in.shape[2]
    assert in_cols % WEIGHT_SLAB == 0 and d_model % WEIGHT_SLAB == 0

    full = lambda shape: pl.BlockSpec(shape, lambda b, s: (0,) * len(shape))
    grid_spec = pltpu.PrefetchScalarGridSpec(
        num_scalar_prefetch=0,
        grid=(bsz, seq // tile),
        in_specs=[
            pl.BlockSpec((None, tile, d_model), lambda b, s: (b, s, 0)),
            full((1, d_model)),
            pl.BlockSpec(memory_space=pltpu.HBM),
            full((depth + 1, D_HGRN)),
            full((CONV_WIDTH, 1, D_CONV)),
            full((1, D_HGRN)),
            full((1, D_CONV)),
            pl.BlockSpec(memory_space=pltpu.HBM),
            full((1, d_model)),
        ],
        out_specs=pl.BlockSpec((None, tile, d_model), lambda b, s: (b, s, 0)),
        scratch_shapes=[
            pltpu.VMEM((N_HEADS, HEAD_DIM, HEAD_DIM), jnp.float32),
            pltpu.VMEM((tile + SUBLANES, D_CONV), jnp.float32),
            pltpu.VMEM((tile, D_HGRN), jnp.float32),
            pltpu.VMEM((d_model, in_cols), jnp.bfloat16),
            pltpu.VMEM((D_HGRN + D_CONV, d_model), jnp.bfloat16),
            pltpu.VMEM((2, d_model, WEIGHT_SLAB), jnp.float32),
            pltpu.SemaphoreType.DMA((2,)),
        ],
    )
    return pl.pallas_call(
        _block_kernel,
        grid_spec=grid_spec,
        out_shape=jax.ShapeDtypeStruct(x.shape, x.dtype),
        compiler_params=pltpu.CompilerParams(
            dimension_semantics=("arbitrary", "arbitrary"),
            vmem_limit_bytes=VMEM_LIMIT_BYTES),
        name="hgrn2_shortconv_block",
    )(x, norm_gain, w_in, lb_logits, jnp.transpose(conv_w, (1, 0, 2)), hgrn_norm_gain,
      conv_norm_gain, w_out, final_norm_gain.reshape(1, d_model))
```

```python
import jax
import jax.numpy as jnp
from jax import lax
from jax.experimental import pallas as pl
from jax.experimental.pallas import tpu as pltpu

D_HGRN = 512
D_CONV = 512
HEAD_DIM = 128
N_HEADS = D_HGRN // HEAD_DIM
CHUNK = 64
BLOCK = 2 * CHUNK
CONV_WIDTH = 3
CONV_GROUP_DIM = 64
EPS = 1e-6
SEQ_TILE = 1024
SUB_TILE = 1024
LANES = 128
SUBLANES = 8
VMEM_LIMIT_BYTES = 60000 * 1024
WEIGHT_SLAB = 512

_TN = (((0,), (0,)), ((), ()))


def _silu(z):
    hz = 0.5 * z
    return hz + hz * jnp.tanh(hz)


def _slab(x, r0, lb, g_ref, win_ref, cw_ref, hg_ref, cg_ref, wout_ref, fg_ref,
          st_ref, cu_ref, o_s, loader=None):
    f32, bf16 = jnp.float32, jnp.bfloat16
    n_rows = x.shape[0]

    ms = jnp.mean(x * x, axis=-1, keepdims=True)
    h = (x * lax.rsqrt(ms + EPS) * g_ref[...]).astype(bf16)

    def proj(j, width):
        if loader is not None:
            loader.need_w_in(j)
        return jnp.dot(h, win_ref[:, j * width:(j + 1) * width], preferred_element_type=f32)

    q = proj(0, D_HGRN)
    f = (0.5 + 0.5 * lb) + (0.5 - 0.5 * lb) * jnp.tanh(0.5 * proj(1, D_HGRN))
    lf = jnp.log(f)
    k = 1.0 - f
    v = proj(2, D_HGRN).astype(bf16)
    u = proj(4, D_CONV)

    row = lax.broadcasted_iota(jnp.int32, (BLOCK, BLOCK), 0)
    col = lax.broadcasted_iota(jnp.int32, (BLOCK, BLOCK), 1)
    causal = row >= col
    rel = (((col >= CHUNK) & (col <= row)).astype(f32)
           - ((col < CHUNK) & (col > row)).astype(f32)).astype(bf16)
    rel2 = jnp.concatenate([rel, rel], axis=1)
    n_blocks = n_rows // BLOCK
    rows = [slice(c * BLOCK, (c + 1) * BLOCK) for c in range(n_blocks)]
    heads = [slice(hd * HEAD_DIM, (hd + 1) * HEAD_DIM) for hd in range(N_HEADS)]

    lf_hi = lf.astype(bf16)
    lf_lo = (lf - lf_hi.astype(f32)).astype(bf16)
    b = jnp.concatenate(
        [jnp.dot(rel2, jnp.concatenate([lf_hi[r], lf_lo[r]], axis=0),
                 preferred_element_type=f32) for r in rows], axis=0)
    gate_c = proj(6, D_CONV)
    eb = jnp.exp(b)
    eib = jnp.exp(-b)
    q_dec = (q * eb).astype(bf16)
    k_inv = k * eib
    head = [eib[r][0:1, :] * f[r][0:1, :] for r in rows]
    tail = [eb[r][BLOCK - 1:BLOCK, :] for r in rows]
    k_end = [(k_inv[r] * tail[c]).astype(bf16) for c, r in enumerate(rows)]

    scores = [[jnp.dot(q_dec[r, sl], k_inv[r, sl].T.astype(bf16), preferred_element_type=f32)
               for sl in heads] for r in rows]
    upd_t = [[lax.dot_general(v[r, sl], k_end[c][:, sl], _TN, preferred_element_type=f32)
              for sl in heads] for c, r in enumerate(rows)]
    gate_b = proj(5, D_CONV)
    state_before = [[None] * N_HEADS for _ in rows]
    for hd, sl in enumerate(heads):
        state_t = st_ref[hd]
        for c in range(n_blocks):
            state_before[c][hd] = (state_t * head[c][:, sl]).T.astype(bf16)
            state_t = state_t * (head[c][:, sl] * tail[c][:, sl]) + upd_t[c][hd]
        st_ref[hd] = state_t
    for c, r in enumerate(rows):
        for hd, sl in enumerate(heads):
            sc = jnp.where(causal, scores[c][hd], 0.0).astype(bf16)
            o_s[r0 + c * BLOCK:r0 + (c + 1) * BLOCK, sl] = (
                jnp.dot(sc, v[r, sl], preferred_element_type=f32)
                + jnp.dot(q_dec[r, sl], state_before[c][hd], preferred_element_type=f32))
    z_b = proj(7, D_CONV)
    z_a = proj(3, D_HGRN)

    hg = hg_ref[...]
    o_parts = []
    for sl in heads:
        oh = o_s[r0:r0 + n_rows, sl]
        oms = jnp.mean(oh * oh, axis=-1, keepdims=True)
        o_parts.append(oh * lax.rsqrt(oms + EPS) * hg[:, sl])
    o_a = (jnp.concatenate(o_parts, axis=-1) * _silu(z_a)).astype(bf16)

    cw = [cw_ref[j] for j in range(CONV_WIDTH)]
    cu = gate_c * u
    c0 = r0 + SUBLANES
    cu_ref[c0:c0 + n_rows, :] = cu
    conv = (cw[2] * cu
            + cw[1] * cu_ref[c0 - 1:c0 - 1 + n_rows, :]
            + cw[0] * cu_ref[c0 - 2:c0 - 2 + n_rows, :])
    y = gate_b * conv
    cg = cg_ref[...]
    lane = lax.broadcasted_iota(jnp.int32, (1, LANES), 1)
    first_group = lane < CONV_GROUP_DIM
    y_parts = []
    for j in range(D_CONV // LANES):
        sl = slice(j * LANES, (j + 1) * LANES)
        yj = y[:, sl]
        y2 = yj * yj
        tot = jnp.sum(y2, axis=-1, keepdims=True)
        lo = jnp.sum(jnp.where(first_group, y2, 0.0), axis=-1, keepdims=True)
        yms = jnp.where(first_group, lo, tot - lo) * (1.0 / CONV_GROUP_DIM)
        y_parts.append(yj * lax.rsqrt(yms + EPS) * cg[:, sl])
    o_b = (jnp.concatenate(y_parts, axis=-1) * _silu(z_b)).astype(bf16)

    if loader is not None:
        loader.need_all()
    mix = (jnp.dot(o_a, wout_ref[0:D_HGRN, :], preferred_element_type=f32)
           + jnp.dot(o_b, wout_ref[D_HGRN:D_HGRN + D_CONV, :], preferred_element_type=f32))
    res = x + mix
    rms = jnp.mean(res * res, axis=-1, keepdims=True)
    return res * lax.rsqrt(rms + EPS) * fg_ref[...]


W_IN_USE_ORDER = (0, 1, 2, 4, 6, 5, 7, 3)


class _WeightLoader:
    def __init__(self, win_hbm, wout_hbm, win_ref, wout_ref, stage, sems):
        assert win_ref.shape[1] == len(W_IN_USE_ORDER) * WEIGHT_SLAB
        cols = lambda j: pl.ds(j * WEIGHT_SLAB, WEIGHT_SLAB)
        self.slabs = [(win_hbm.at[0, :, cols(j)], win_ref.at[:, cols(j)]) for j in W_IN_USE_ORDER]
        self.slabs += [(wout_hbm.at[0, :, cols(j)], wout_ref.at[:, cols(j)])
                       for j in range(wout_ref.shape[1] // WEIGHT_SLAB)]
        self.stage, self.sems = stage, sems
        self.n_ready = 0
        for i in range(min(2, len(self.slabs))):
            self._copy(i).start()

    def _copy(self, i):
        return pltpu.make_async_copy(self.slabs[i][0], self.stage.at[i % 2], self.sems.at[i % 2])

    def _need(self, n):
        while self.n_ready < n:
            i = self.n_ready
            self._copy(i).wait()
            self.slabs[i][1][...] = self.stage[i % 2].astype(jnp.bfloat16)
            if i + 2 < len(self.slabs):
                self._copy(i + 2).start()
            self.n_ready += 1

    def need_w_in(self, j):
        self._need(W_IN_USE_ORDER.index(j) + 1)

    def need_all(self):
        self._need(len(self.slabs))


def _block_kernel(x_ref, g_ref, win_hbm, lbl_ref, cw_ref, hg_ref, cg_ref, wout_hbm, fg_ref,
                  out_ref, st_ref, cu_ref, o_s, win_ref, wout_ref, stage, sems):
    tile = x_ref.shape[0]
    sub = min(SUB_TILE, tile)

    @pl.when(pl.program_id(1) == 0)
    def _reset_carries():
        st_ref[...] = jnp.zeros_like(st_ref)
        cu_ref[0:SUBLANES, :] = jnp.zeros((SUBLANES, D_CONV), jnp.float32)

    def run_step(loader):
        lbl = lbl_ref[...]
        lmax = jnp.max(lbl, axis=0, keepdims=True)
        lexp = jnp.exp(lbl - lmax)
        lb = lexp[0:1, :] / jnp.sum(lexp, axis=0, keepdims=True)
        for r0 in range(0, tile, sub):
            out_ref[r0:r0 + sub, :] = _slab(
                x_ref[r0:r0 + sub, :], r0, lb, g_ref, win_ref, cw_ref, hg_ref, cg_ref, wout_ref,
                fg_ref, st_ref, cu_ref, o_s, loader if r0 == 0 else None)
        cu_ref[0:SUBLANES, :] = cu_ref[tile:tile + SUBLANES, :]

    first_step = (pl.program_id(0) == 0) & (pl.program_id(1) == 0)

    @pl.when(first_step)
    def _step_with_weight_load():
        run_step(_WeightLoader(win_hbm, wout_hbm, win_ref, wout_ref, stage, sems))

    @pl.when(jnp.logical_not(first_step))
    def _step():
        run_step(None)


@jax.jit
def kernel(x, norm_gain, w_in, lb_logits, conv_w, hgrn_norm_gain, conv_norm_gain, w_out,
           final_norm_gain):
    bsz, seq, d_model = x.shape
    depth = norm_gain.shape[0]
    assert depth == 1 and lb_logits.shape == (depth + 1, D_HGRN)
    assert w_in.shape == (depth, d_model, 4 * D_HGRN + 4 * D_CONV)
    assert w_out.shape == (depth, D_HGRN + D_CONV, d_model)
    tile = min(SEQ_TILE, seq)
    sub = min(SUB_TILE, tile)
    assert seq % tile == 0 and tile % sub == 0 and sub % BLOCK == 0
    in_cols = w_in.shape[2]
    assert in_cols % WEIGHT_SLAB == 0 and d_model % WEIGHT_SLAB == 0

    full = lambda shape: pl.BlockSpec(shape, lambda b, s: (0,) * len(shape))
    grid_spec = pltpu.PrefetchScalarGridSpec(
        num_scalar_prefetch=0,
        grid=(bsz, seq // tile),
        in_specs=[
            pl.BlockSpec((None, tile, d_model), lambda b, s: (b, s, 0)),
            full((1, d_model)),
            pl.BlockSpec(memory_space=pltpu.HBM),
            full((depth + 1, D_HGRN)),
            full((CONV_WIDTH, 1, D_CONV)),
            full((1, D_HGRN)),
            full((1, D_CONV)),
            pl.BlockSpec(memory_space=pltpu.HBM),
            full((1, d_model)),
        ],
        out_specs=pl.BlockSpec((None, tile, d_model), lambda b, s: (b, s, 0)),
        scratch_shapes=[
            pltpu.VMEM((N_HEADS, HEAD_DIM, HEAD_DIM), jnp.float32),
            pltpu.VMEM((tile + SUBLANES, D_CONV), jnp.float32),
            pltpu.VMEM((tile, D_HGRN), jnp.float32),
            pltpu.VMEM((d_model, in_cols), jnp.bfloat16),
            pltpu.VMEM((D_HGRN + D_CONV, d_model), jnp.bfloat16),
            pltpu.VMEM((2, d_model, WEIGHT_SLAB), jnp.float32),
            pltpu.SemaphoreType.DMA((2,)),
        ],
    )
    return pl.pallas_call(
        _block_kernel,
        grid_spec=grid_spec,
        out_shape=jax.ShapeDtypeStruct(x.shape, x.dtype),
        compiler_params=pltpu.CompilerParams(
            dimension_semantics=("arbitrary", "arbitrary"),
            vmem_limit_bytes=VMEM_LIMIT_BYTES),
        name="hgrn2_shortconv_block",
    )(x, norm_gain, w_in, lb_logits, jnp.transpose(conv_w, (1, 0, 2)), hgrn_norm_gain,
      conv_norm_gain, w_out, final_norm_gain.reshape(1, d_model))
```

```python
import jax
import jax.numpy as jnp
from jax import lax
from jax.experimental import pallas as pl
from jax.experimental.pallas import tpu as pltpu

D_HGRN = 512
D_CONV = 512
HEAD_DIM = 128
N_HEADS = D_HGRN // HEAD_DIM
CHUNK = 64
BLOCK = 2 * CHUNK
CONV_WIDTH = 3
CONV_GROUP_DIM = 64
EPS = 1e-6
SEQ_TILE = 1024
SUB_TILE = 1024
LANES = 128
SUBLANES = 8
VMEM_LIMIT_BYTES = 60000 * 1024
WEIGHT_SLAB = 512

_TN = (((0,), (0,)), ((), ()))


def _silu(z):
    hz = 0.5 * z
    return hz + hz * jnp.tanh(hz)


def _slab(x, r0, lb, g_ref, win_ref, cw_ref, hg_ref, cg_ref, wout_ref, fg_ref,
          st_ref, cu_ref, o_s, loader=None):
    f32, bf16 = jnp.float32, jnp.bfloat16
    n_rows = x.shape[0]

    ms = jnp.mean(x * x, axis=-1, keepdims=True)
    h = (x * lax.rsqrt(ms + EPS) * g_ref[...]).astype(bf16)

    def proj(j, width):
        if loader is not None:
            loader.need_w_in(j)
        return jnp.dot(h, win_ref[:, j * width:(j + 1) * width], preferred_element_type=f32)

    q = proj(0, D_HGRN)
    f = (0.5 + 0.5 * lb) + (0.5 - 0.5 * lb) * jnp.tanh(0.5 * proj(1, D_HGRN))
    lf = jnp.log(f)
    k = 1.0 - f
    v = proj(2, D_HGRN).astype(bf16)
    u = proj(4, D_CONV)

    row = lax.broadcasted_iota(jnp.int32, (BLOCK, BLOCK), 0)
    col = lax.broadcasted_iota(jnp.int32, (BLOCK, BLOCK), 1)
    causal = row >= col
    rel = (((col >= CHUNK) & (col <= row)).astype(f32)
           - ((col < CHUNK) & (col > row)).astype(f32)).astype(bf16)
    rel2 = jnp.concatenate([rel, rel], axis=1)
    n_blocks = n_rows // BLOCK
    rows = [slice(c * BLOCK, (c + 1) * BLOCK) for c in range(n_blocks)]
    heads = [slice(hd * HEAD_DIM, (hd + 1) * HEAD_DIM) for hd in range(N_HEADS)]

    lf_hi = lf.astype(bf16)
    lf_lo = (lf - lf_hi.astype(f32)).astype(bf16)
    b = jnp.concatenate(
        [jnp.dot(rel2, jnp.concatenate([lf_hi[r], lf_lo[r]], axis=0),
                 preferred_element_type=f32) for r in rows], axis=0)
    gate_c = proj(6, D_CONV)
    eb = jnp.exp(b)
    eib = jnp.exp(-b)
    q_dec = (q * eb).astype(bf16)
    k_inv = k * eib
    head = [eib[r][0:1, :] * f[r][0:1, :] for r in rows]
    tail = [eb[r][BLOCK - 1:BLOCK, :] for r in rows]
    k_end = [(k_inv[r] * tail[c]).astype(bf16) for c, r in enumerate(rows)]

    scores = [[jnp.dot(q_dec[r, sl], k_inv[r, sl].T.astype(bf16), preferred_element_type=f32)
               for sl in heads] for r in rows]
    upd_t = [[lax.dot_general(v[r, sl], k_end[c][:, sl], _TN, preferred_element_type=f32)
              for sl in heads] for c, r in enumerate(rows)]
    gate_b = proj(5, D_CONV)
    state_before = [[None] * N_HEADS for _ in rows]
    for hd, sl in enumerate(heads):
        state_t = st_ref[hd]
        for c in range(n_blocks):
            state_before[c][hd] = (state_t * head[c][:, sl]).T.astype(bf16)
            state_t = state_t * (head[c][:, sl] * tail[c][:, sl]) + upd_t[c][hd]
        st_ref[hd] = state_t
    gate_a = _silu(proj(3, D_HGRN))
    hg = hg_ref[...]
    for c, r in enumerate(rows):
        for hd, sl in enumerate(heads):
            sc = jnp.where(causal, scores[c][hd], 0.0).astype(bf16)
            oh = (jnp.dot(sc, v[r, sl], preferred_element_type=f32)
                  + jnp.dot(q_dec[r, sl], state_before[c][hd], preferred_element_type=f32))
            oms = jnp.mean(oh * oh, axis=-1, keepdims=True)
            o_s[r0 + c * BLOCK:r0 + (c + 1) * BLOCK, sl] = (
                oh * lax.rsqrt(oms + EPS) * hg[:, sl] * gate_a[r, sl]).astype(bf16)
    z_b = proj(7, D_CONV)
    o_a = o_s[r0:r0 + n_rows, :]

    cw = [cw_ref[j] for j in range(CONV_WIDTH)]
    cu = gate_c * u
    c0 = r0 + SUBLANES
    cu_ref[c0:c0 + n_rows, :] = cu
    conv = (cw[2] * cu
            + cw[1] * cu_ref[c0 - 1:c0 - 1 + n_rows, :]
            + cw[0] * cu_ref[c0 - 2:c0 - 2 + n_rows, :])
    y = gate_b * conv
    cg = cg_ref[...]
    lane = lax.broadcasted_iota(jnp.int32, (1, LANES), 1)
    first_group = lane < CONV_GROUP_DIM
    y_parts = []
    for j in range(D_CONV // LANES):
        sl = slice(j * LANES, (j + 1) * LANES)
        yj = y[:, sl]
        y2 = yj * yj
        tot = jnp.sum(y2, axis=-1, keepdims=True)
        lo = jnp.sum(jnp.where(first_group, y2, 0.0), axis=-1, keepdims=True)
        yms = jnp.where(first_group, lo, tot - lo) * (1.0 / CONV_GROUP_DIM)
        y_parts.append(yj * lax.rsqrt(yms + EPS) * cg[:, sl])
    o_b = (jnp.concatenate(y_parts, axis=-1) * _silu(z_b)).astype(bf16)

    if loader is not None:
        loader.need_all()
    mix = (jnp.dot(o_a, wout_ref[0:D_HGRN, :], preferred_element_type=f32)
           + jnp.dot(o_b, wout_ref[D_HGRN:D_HGRN + D_CONV, :], preferred_element_type=f32))
    res = x + mix
    rms = jnp.mean(res * res, axis=-1, keepdims=True)
    return res * lax.rsqrt(rms + EPS) * fg_ref[...]


W_IN_USE_ORDER = (0, 1, 2, 4, 6, 5, 3, 7)


class _WeightLoader:
    def __init__(self, win_hbm, wout_hbm, win_ref, wout_ref, stage, sems):
        assert win_ref.shape[1] == len(W_IN_USE_ORDER) * WEIGHT_SLAB
        cols = lambda j: pl.ds(j * WEIGHT_SLAB, WEIGHT_SLAB)
        self.slabs = [(win_hbm.at[0, :, cols(j)], win_ref.at[:, cols(j)]) for j in W_IN_USE_ORDER]
        self.slabs += [(wout_hbm.at[0, :, cols(j)], wout_ref.at[:, cols(j)])
                       for j in range(wout_ref.shape[1] // WEIGHT_SLAB)]
        self.stage, self.sems = stage, sems
        self.n_ready = 0
        for i in range(min(2, len(self.slabs))):
            self._copy(i).start()

    def _copy(self, i):
        return pltpu.make_async_copy(self.slabs[i][0], self.stage.at[i % 2], self.sems.at[i % 2])

    def _need(self, n):
        while self.n_ready < n:
            i = self.n_ready
            self._copy(i).wait()
            self.slabs[i][1][...] = self.stage[i % 2].astype(jnp.bfloat16)
            if i + 2 < len(self.slabs):
                self._copy(i + 2).start()
            self.n_ready += 1

    def need_w_in(self, j):
        self._need(W_IN_USE_ORDER.index(j) + 1)

    def need_all(self):
        self._need(len(self.slabs))


def _block_kernel(x_ref, g_ref, win_hbm, lbl_ref, cw_ref, hg_ref, cg_ref, wout_hbm, fg_ref,
                  out_ref, st_ref, cu_ref, o_s, win_ref, wout_ref, stage, sems):
    tile = x_ref.shape[0]
    sub = min(SUB_TILE, tile)

    @pl.when(pl.program_id(1) == 0)
    def _reset_carries():
        st_ref[...] = jnp.zeros_like(st_ref)
        cu_ref[0:SUBLANES, :] = jnp.zeros((SUBLANES, D_CONV), jnp.float32)

    def run_step(loader):
        lbl = lbl_ref[...]
        lmax = jnp.max(lbl, axis=0, keepdims=True)
        lexp = jnp.exp(lbl - lmax)
        lb = lexp[0:1, :] / jnp.sum(lexp, axis=0, keepdims=True)
        for r0 in range(0, tile, sub):
            out_ref[r0:r0 + sub, :] = _slab(
                x_ref[r0:r0 + sub, :], r0, lb, g_ref, win_ref, cw_ref, hg_ref, cg_ref, wout_ref,
                fg_ref, st_ref, cu_ref, o_s, loader if r0 == 0 else None)
        cu_ref[0:SUBLANES, :] = cu_ref[tile:tile + SUBLANES, :]

    first_step = (pl.program_id(0) == 0) & (pl.program_id(1) == 0)

    @pl.when(first_step)
    def _step_with_weight_load():
        run_step(_WeightLoader(win_hbm, wout_hbm, win_ref, wout_ref, stage, sems))

    @pl.when(jnp.logical_not(first_step))
    def _step():
        run_step(None)


@jax.jit
def kernel(x, norm_gain, w_in, lb_logits, conv_w, hgrn_norm_gain, conv_norm_gain, w_out,
           final_norm_gain):
    bsz, seq, d_model = x.shape
    depth = norm_gain.shape[0]
    assert depth == 1 and lb_logits.shape == (depth + 1, D_HGRN)
    assert w_in.shape == (depth, d_model, 4 * D_HGRN + 4 * D_CONV)
    assert w_out.shape == (depth, D_HGRN + D_CONV, d_model)
    tile = min(SEQ_TILE, seq)
    sub = min(SUB_TILE, tile)
    assert seq % tile == 0 and tile % sub == 0 and sub % BLOCK == 0
    in_cols = w_in.shape[2]
    assert in_cols % WEIGHT_SLAB == 0 and d_model % WEIGHT_SLAB == 0

    full = lambda shape: pl.BlockSpec(shape, lambda b, s: (0,) * len(shape))
    grid_spec = pltpu.PrefetchScalarGridSpec(
        num_scalar_prefetch=0,
        grid=(bsz, seq // tile),
        in_specs=[
            pl.BlockSpec((None, tile, d_model), lambda b, s: (b, s, 0)),
            full((1, d_model)),
            pl.BlockSpec(memory_space=pltpu.HBM),
            full((depth + 1, D_HGRN)),
            full((CONV_WIDTH, 1, D_CONV)),
            full((1, D_HGRN)),
            full((1, D_CONV)),
            pl.BlockSpec(memory_space=pltpu.HBM),
            full((1, d_model)),
        ],
        out_specs=pl.BlockSpec((None, tile, d_model), lambda b, s: (b, s, 0)),
        scratch_shapes=[
            pltpu.VMEM((N_HEADS, HEAD_DIM, HEAD_DIM), jnp.float32),
            pltpu.VMEM((tile + SUBLANES, D_CONV), jnp.float32),
            pltpu.VMEM((tile, D_HGRN), jnp.bfloat16),
            pltpu.VMEM((d_model, in_cols), jnp.bfloat16),
            pltpu.VMEM((D_HGRN + D_CONV, d_model), jnp.bfloat16),
            pltpu.VMEM((2, d_model, WEIGHT_SLAB), jnp.float32),
            pltpu.SemaphoreType.DMA((2,)),
        ],
    )
    return pl.pallas_call(
        _block_kernel,
        grid_spec=grid_spec,
        out_shape=jax.ShapeDtypeStruct(x.shape, x.dtype),
        compiler_params=pltpu.CompilerParams(
            dimension_semantics=("arbitrary", "arbitrary"),
            vmem_limit_bytes=VMEM_LIMIT_BYTES),
        name="hgrn2_shortconv_block",
    )(x, norm_gain, w_in, lb_logits, jnp.transpose(conv_w, (1, 0, 2)), hgrn_norm_gain,
      conv_norm_gain, w_out, final_norm_gain.reshape(1, d_model))
```

```python
import jax
import jax.numpy as jnp
from jax import lax
from jax.experimental import pallas as pl
from jax.experimental.pallas import tpu as pltpu

D_HGRN = 512
D_CONV = 512
HEAD_DIM = 128
N_HEADS = D_HGRN // HEAD_DIM
CHUNK = 64
BLOCK = 2 * CHUNK
CONV_WIDTH = 3
CONV_GROUP_DIM = 64
EPS = 1e-6
SEQ_TILE = 1024
SUB_TILE = 1024
LANES = 128
SUBLANES = 8
VMEM_LIMIT_BYTES = 60000 * 1024
WEIGHT_SLAB = 512

_TN = (((0,), (0,)), ((), ()))


def _silu(z):
    hz = 0.5 * z
    return hz + hz * jnp.tanh(hz)


def _slab(x, r0, lb, g_ref, win_ref, cw_ref, hg_ref, cg_ref, wout_ref, fg_ref,
          st_ref, cu_ref, o_s, loader=None):
    f32, bf16 = jnp.float32, jnp.bfloat16
    n_rows = x.shape[0]

    ms = jnp.mean(x * x, axis=-1, keepdims=True)
    h = (x * lax.rsqrt(ms + EPS) * g_ref[...]).astype(bf16)

    def proj(j, width):
        if loader is not None:
            loader.need_w_in(j)
        return jnp.dot(h, win_ref[:, j * width:(j + 1) * width], preferred_element_type=f32)

    q = proj(0, D_HGRN)
    f = (0.5 + 0.5 * lb) + (0.5 - 0.5 * lb) * jnp.tanh(0.5 * proj(1, D_HGRN))
    lf = jnp.log(f)
    k = 1.0 - f
    v = proj(2, D_HGRN).astype(bf16)

    row = lax.broadcasted_iota(jnp.int32, (BLOCK, BLOCK), 0)
    col = lax.broadcasted_iota(jnp.int32, (BLOCK, BLOCK), 1)
    causal = row >= col
    rel = (((col >= CHUNK) & (col <= row)).astype(f32)
           - ((col < CHUNK) & (col > row)).astype(f32)).astype(bf16)
    rel2 = jnp.concatenate([rel, rel], axis=1)
    n_blocks = n_rows // BLOCK
    rows = [slice(c * BLOCK, (c + 1) * BLOCK) for c in range(n_blocks)]
    heads = [slice(hd * HEAD_DIM, (hd + 1) * HEAD_DIM) for hd in range(N_HEADS)]

    lf_hi = lf.astype(bf16)
    lf_lo = (lf - lf_hi.astype(f32)).astype(bf16)
    b = jnp.concatenate(
        [jnp.dot(rel2, jnp.concatenate([lf_hi[r], lf_lo[r]], axis=0),
                 preferred_element_type=f32) for r in rows], axis=0)
    eb = jnp.exp(b)
    eib = jnp.exp(-b)
    q_dec = (q * eb).astype(bf16)
    k_inv = k * eib
    head = [eib[r][0:1, :] * f[r][0:1, :] for r in rows]
    tail = [eb[r][BLOCK - 1:BLOCK, :] for r in rows]
    k_end = [(k_inv[r] * tail[c]).astype(bf16) for c, r in enumerate(rows)]

    scores = [[jnp.dot(q_dec[r, sl], k_inv[r, sl].T.astype(bf16), preferred_element_type=f32)
               for sl in heads] for r in rows]
    upd_t = [[lax.dot_general(v[r, sl], k_end[c][:, sl], _TN, preferred_element_type=f32)
              for sl in heads] for c, r in enumerate(rows)]
    state_before = [[None] * N_HEADS for _ in rows]
    for hd, sl in enumerate(heads):
        state_t = st_ref[hd]
        for c in range(n_blocks):
            state_before[c][hd] = (state_t * head[c][:, sl]).T.astype(bf16)
            state_t = state_t * (head[c][:, sl] * tail[c][:, sl]) + upd_t[c][hd]
        st_ref[hd] = state_t
    for c, r in enumerate(rows):
        for hd, sl in enumerate(heads):
            sc = jnp.where(causal, scores[c][hd], 0.0).astype(bf16)
            o_s[r0 + c * BLOCK:r0 + (c + 1) * BLOCK, sl] = (
                jnp.dot(sc, v[r, sl], preferred_element_type=f32)
                + jnp.dot(q_dec[r, sl], state_before[c][hd], preferred_element_type=f32))
    z_a = proj(3, D_HGRN)

    hg = hg_ref[...]
    o_parts = []
    for sl in heads:
        oh = o_s[r0:r0 + n_rows, sl]
        oms = jnp.mean(oh * oh, axis=-1, keepdims=True)
        o_parts.append(oh * lax.rsqrt(oms + EPS) * hg[:, sl])
    o_a = (jnp.concatenate(o_parts, axis=-1) * _silu(z_a)).astype(bf16)

    cw = [cw_ref[j] for j in range(CONV_WIDTH)]
    cu = proj(6, D_CONV) * proj(4, D_CONV)
    c0 = r0 + SUBLANES
    cu_ref[c0:c0 + n_rows, :] = cu
    conv = (cw[2] * cu
            + cw[1] * cu_ref[c0 - 1:c0 - 1 + n_rows, :]
            + cw[0] * cu_ref[c0 - 2:c0 - 2 + n_rows, :])
    y = proj(5, D_CONV) * conv
    cg = cg_ref[...]
    lane = lax.broadcasted_iota(jnp.int32, (1, LANES), 1)
    first_group = lane < CONV_GROUP_DIM
    y_parts = []
    for j in range(D_CONV // LANES):
        sl = slice(j * LANES, (j + 1) * LANES)
        yj = y[:, sl]
        y2 = yj * yj
        tot = jnp.sum(y2, axis=-1, keepdims=True)
        lo = jnp.sum(jnp.where(first_group, y2, 0.0), axis=-1, keepdims=True)
        yms = jnp.where(first_group, lo, tot - lo) * (1.0 / CONV_GROUP_DIM)
        y_parts.append(yj * lax.rsqrt(yms + EPS) * cg[:, sl])
    o_b = (jnp.concatenate(y_parts, axis=-1) * _silu(proj(7, D_CONV))).astype(bf16)

    if loader is not None:
        loader.need_all()
    mix = (jnp.dot(o_a, wout_ref[0:D_HGRN, :], preferred_element_type=f32)
           + jnp.dot(o_b, wout_ref[D_HGRN:D_HGRN + D_CONV, :], preferred_element_type=f32))
    res = x + mix
    rms = jnp.mean(res * res, axis=-1, keepdims=True)
    return res * lax.rsqrt(rms + EPS) * fg_ref[...]


W_IN_USE_ORDER = (0, 1, 2, 3, 6, 4, 5, 7)


class _WeightLoader:
    def __init__(self, win_hbm, wout_hbm, win_ref, wout_ref, stage, sems):
        assert win_ref.shape[1] == len(W_IN_USE_ORDER) * WEIGHT_SLAB
        cols = lambda j: pl.ds(j * WEIGHT_SLAB, WEIGHT_SLAB)
        self.slabs = [(win_hbm.at[0, :, cols(j)], win_ref.at[:, cols(j)]) for j in W_IN_USE_ORDER]
        self.slabs += [(wout_hbm.at[0, :, cols(j)], wout_ref.at[:, cols(j)])
                       for j in range(wout_ref.shape[1] // WEIGHT_SLAB)]
        self.stage, self.sems = stage, sems
        self.n_ready = 0
        for i in range(min(2, len(self.slabs))):
            self._copy(i).start()

    def _copy(self, i):
        return pltpu.make_async_copy(self.slabs[i][0], self.stage.at[i % 2], self.sems.at[i % 2])

    def _need(self, n):
        while self.n_ready < n:
            i = self.n_ready
            self._copy(i).wait()
            self.slabs[i][1][...] = self.stage[i % 2].astype(jnp.bfloat16)
            if i + 2 < len(self.slabs):
                self._copy(i + 2).start()
            self.n_ready += 1

    def need_w_in(self, j):
        self._need(W_IN_USE_ORDER.index(j) + 1)

    def need_all(self):
        self._need(len(self.slabs))


def _block_kernel(x_ref, g_ref, win_hbm, lbl_ref, cw_ref, hg_ref, cg_ref, wout_hbm, fg_ref,
                  out_ref, st_ref, cu_ref, o_s, win_ref, wout_ref, stage, sems):
    tile = x_ref.shape[0]
    sub = min(SUB_TILE, tile)

    @pl.when(pl.program_id(1) == 0)
    def _reset_carries():
        st_ref[...] = jnp.zeros_like(st_ref)
        cu_ref[0:SUBLANES, :] = jnp.zeros((SUBLANES, D_CONV), jnp.float32)

    def run_step(loader):
        lbl = lbl_ref[...]
        lmax = jnp.max(lbl, axis=0, keepdims=True)
        lexp = jnp.exp(lbl - lmax)
        lb = lexp[0:1, :] / jnp.sum(lexp, axis=0, keepdims=True)
        for r0 in range(0, tile, sub):
            out_ref[r0:r0 + sub, :] = _slab(
                x_ref[r0:r0 + sub, :], r0, lb, g_ref, win_ref, cw_ref, hg_ref, cg_ref, wout_ref,
                fg_ref, st_ref, cu_ref, o_s, loader if r0 == 0 else None)
        cu_ref[0:SUBLANES, :] = cu_ref[tile:tile + SUBLANES, :]

    first_step = (pl.program_id(0) == 0) & (pl.program_id(1) == 0)

    @pl.when(first_step)
    def _step_with_weight_load():
        run_step(_WeightLoader(win_hbm, wout_hbm, win_ref, wout_ref, stage, sems))

    @pl.when(jnp.logical_not(first_step))
    def _step():
        run_step(None)


@jax.jit
def kernel(x, norm_gain, w_in, lb_logits, conv_w, hgrn_norm_gain, conv_norm_gain, w_out,
           final_norm_gain):
    bsz, seq, d_model = x.shape
    depth = norm_gain.shape[0]
    assert depth == 1 and lb_logits.shape == (depth + 1, D_HGRN)
    assert w_in.shape == (depth, d_model, 4 * D_HGRN + 4 * D_CONV)
    assert w_out.shape == (depth, D_HGRN + D_CONV, d_model)
    tile = min(SEQ_TILE, seq)
    sub = min(SUB_TILE, tile)
    assert seq % tile == 0 and tile % sub == 0 and sub % BLOCK == 0
    in_cols = w_in.shape[2]
    assert in_cols % WEIGHT_SLAB == 0 and d_model % WEIGHT_SLAB == 0

    full = lambda shape: pl.BlockSpec(shape, lambda b, s: (0,) * len(shape))
    grid_spec = pltpu.PrefetchScalarGridSpec(
        num_scalar_prefetch=0,
        grid=(bsz, seq // tile),
        in_specs=[
            pl.BlockSpec((None, tile, d_model), lambda b, s: (b, s, 0)),
            full((1, d_model)),
            pl.BlockSpec(memory_space=pltpu.HBM),
            full((depth + 1, D_HGRN)),
            full((CONV_WIDTH, 1, D_CONV)),
            full((1, D_HGRN)),
            full((1, D_CONV)),
            pl.BlockSpec(memory_space=pltpu.HBM),
            full((1, d_model)),
        ],
        out_specs=pl.BlockSpec((None, tile, d_model), lambda b, s: (b, s, 0)),
        scratch_shapes=[
            pltpu.VMEM((N_HEADS, HEAD_DIM, HEAD_DIM), jnp.float32),
            pltpu.VMEM((tile + SUBLANES, D_CONV), jnp.float32),
            pltpu.VMEM((tile, D_HGRN), jnp.float32),
            pltpu.VMEM((d_model, in_cols), jnp.bfloat16),
            pltpu.VMEM((D_HGRN + D_CONV, d_model), jnp.bfloat16),
            pltpu.VMEM((2, d_model, WEIGHT_SLAB), jnp.float32),
            pltpu.SemaphoreType.DMA((2,)),
        ],
    )
    return pl.pallas_call(
        _block_kernel,
        grid_spec=grid_spec,
        out_shape=jax.ShapeDtypeStruct(x.shape, x.dtype),
        compiler_params=pltpu.CompilerParams(
            dimension_semantics=("arbitrary", "arbitrary"),
            vmem_limit_bytes=VMEM_LIMIT_BYTES),
        name="hgrn2_shortconv_block",
    )(x, norm_gain, w_in, lb_logits, jnp.transpose(conv_w, (1, 0, 2)), hgrn_norm_gain,
      conv_norm_gain, w_out, final_norm_gain.reshape(1, d_model))
```

```python
import jax
import jax.numpy as jnp
from jax import lax
from jax.experimental import pallas as pl
from jax.experimental.pallas import tpu as pltpu

D_HGRN = 512
D_CONV = 512
HEAD_DIM = 128
N_HEADS = D_HGRN // HEAD_DIM
CHUNK = 64
BLOCK = 2 * CHUNK
CONV_WIDTH = 3
CONV_GROUP_DIM = 64
EPS = 1e-6
SEQ_TILE = 512
BATCH_TILE = 2
LANES = 128
SUBLANES = 8
VMEM_LIMIT_BYTES = 60000 * 1024
WEIGHT_SLAB = 512

_TN = (((0,), (0,)), ((), ()))


def _silu(z):
    hz = 0.5 * z
    return hz + hz * jnp.tanh(hz)


def _slab(x, n_seq, lb, g_ref, win_ref, cw_ref, hg_ref, cg_ref, wout_ref, fg_ref,
          st_ref, cu_ref, o_s, loader=None):
    f32, bf16 = jnp.float32, jnp.bfloat16
    n_rows = x.shape[0]
    frames = n_rows // n_seq

    ms = jnp.mean(x * x, axis=-1, keepdims=True)
    h = (x * lax.rsqrt(ms + EPS) * g_ref[...]).astype(bf16)

    def proj(j, width):
        if loader is not None:
            loader.need_w_in(j)
        return jnp.dot(h, win_ref[:, j * width:(j + 1) * width], preferred_element_type=f32)

    q = proj(0, D_HGRN)
    f = (0.5 + 0.5 * lb) + (0.5 - 0.5 * lb) * jnp.tanh(0.5 * proj(1, D_HGRN))
    lf = jnp.log(f)
    k = 1.0 - f
    v = proj(2, D_HGRN).astype(bf16)
    u = proj(4, D_CONV)

    row = lax.broadcasted_iota(jnp.int32, (BLOCK, BLOCK), 0)
    col = lax.broadcasted_iota(jnp.int32, (BLOCK, BLOCK), 1)
    causal = row >= col
    rel = (((col >= CHUNK) & (col <= row)).astype(f32)
           - ((col < CHUNK) & (col > row)).astype(f32)).astype(bf16)
    rel2 = jnp.concatenate([rel, rel], axis=1)
    n_blocks = n_rows // BLOCK
    rows = [slice(c * BLOCK, (c + 1) * BLOCK) for c in range(n_blocks)]
    heads = [slice(hd * HEAD_DIM, (hd + 1) * HEAD_DIM) for hd in range(N_HEADS)]

    lf_hi = lf.astype(bf16)
    lf_lo = (lf - lf_hi.astype(f32)).astype(bf16)
    b = jnp.concatenate(
        [jnp.dot(rel2, jnp.concatenate([lf_hi[r], lf_lo[r]], axis=0),
                 preferred_element_type=f32) for r in rows], axis=0)
    gate_c = proj(6, D_CONV)
    eb = jnp.exp(b)
    eib = jnp.exp(-b)
    q_dec = (q * eb).astype(bf16)
    k_inv = k * eib
    head = [eib[r][0:1, :] * f[r][0:1, :] for r in rows]
    tail = [eb[r][BLOCK - 1:BLOCK, :] for r in rows]
    k_end = [(k_inv[r] * tail[c]).astype(bf16) for c, r in enumerate(rows)]

    scores = [[jnp.dot(q_dec[r, sl], k_inv[r, sl].T.astype(bf16), preferred_element_type=f32)
               for sl in heads] for r in rows]
    upd_t = [[lax.dot_general(v[r, sl], k_end[c][:, sl], _TN, preferred_element_type=f32)
              for sl in heads] for c, r in enumerate(rows)]
    gate_b = proj(5, D_CONV)
    state_before = [[None] * N_HEADS for _ in rows]
    blocks_per_seq = n_blocks // n_seq
    for i in range(n_seq):
        for hd, sl in enumerate(heads):
            state_t = st_ref[i, hd]
            for c in range(i * blocks_per_seq, (i + 1) * blocks_per_seq):
                state_before[c][hd] = (state_t * head[c][:, sl]).T.astype(bf16)
                state_t = state_t * (head[c][:, sl] * tail[c][:, sl]) + upd_t[c][hd]
            st_ref[i, hd] = state_t
    for c, r in enumerate(rows):
        for hd, sl in enumerate(heads):
            sc = jnp.where(causal, scores[c][hd], 0.0).astype(bf16)
            o_s[c * BLOCK:(c + 1) * BLOCK, sl] = (
                jnp.dot(sc, v[r, sl], preferred_element_type=f32)
                + jnp.dot(q_dec[r, sl], state_before[c][hd], preferred_element_type=f32))
    z_b = proj(7, D_CONV)
    z_a = proj(3, D_HGRN)

    hg = hg_ref[...]
    o_parts = []
    for sl in heads:
        oh = o_s[:, sl]
        oms = jnp.mean(oh * oh, axis=-1, keepdims=True)
        o_parts.append(oh * lax.rsqrt(oms + EPS) * hg[:, sl])
    o_a = (jnp.concatenate(o_parts, axis=-1) * _silu(z_a)).astype(bf16)

    cw = [cw_ref[j] for j in range(CONV_WIDTH)]
    cu = gate_c * u
    delayed1, delayed2 = [], []
    for i in range(n_seq):
        cu_ref[i, SUBLANES:SUBLANES + frames, :] = cu[i * frames:(i + 1) * frames]
        delayed1.append(cu_ref[i, SUBLANES - 1:SUBLANES - 1 + frames, :])
        delayed2.append(cu_ref[i, SUBLANES - 2:SUBLANES - 2 + frames, :])
        cu_ref[i, 0:SUBLANES, :] = cu_ref[i, frames:frames + SUBLANES, :]
    conv = (cw[2] * cu + cw[1] * jnp.concatenate(delayed1, axis=0)
            + cw[0] * jnp.concatenate(delayed2, axis=0))
    y = gate_b * conv
    cg = cg_ref[...]
    lane = lax.broadcasted_iota(jnp.int32, (1, LANES), 1)
    first_group = lane < CONV_GROUP_DIM
    y_parts = []
    for j in range(D_CONV // LANES):
        sl = slice(j * LANES, (j + 1) * LANES)
        yj = y[:, sl]
        y2 = yj * yj
        tot = jnp.sum(y2, axis=-1, keepdims=True)
        lo = jnp.sum(jnp.where(first_group, y2, 0.0), axis=-1, keepdims=True)
        yms = jnp.where(first_group, lo, tot - lo) * (1.0 / CONV_GROUP_DIM)
        y_parts.append(yj * lax.rsqrt(yms + EPS) * cg[:, sl])
    o_b = (jnp.concatenate(y_parts, axis=-1) * _silu(z_b)).astype(bf16)

    if loader is not None:
        loader.need_all()
    mix = (jnp.dot(o_a, wout_ref[0:D_HGRN, :], preferred_element_type=f32)
           + jnp.dot(o_b, wout_ref[D_HGRN:D_HGRN + D_CONV, :], preferred_element_type=f32))
    res = x + mix
    rms = jnp.mean(res * res, axis=-1, keepdims=True)
    return res * lax.rsqrt(rms + EPS) * fg_ref[...]


W_IN_USE_ORDER = (0, 1, 2, 4, 6, 5, 7, 3)


class _WeightLoader:
    def __init__(self, win_hbm, wout_hbm, win_ref, wout_ref, stage, sems):
        assert win_ref.shape[1] == len(W_IN_USE_ORDER) * WEIGHT_SLAB
        cols = lambda j: pl.ds(j * WEIGHT_SLAB, WEIGHT_SLAB)
        self.slabs = [(win_hbm.at[0, :, cols(j)], win_ref.at[:, cols(j)]) for j in W_IN_USE_ORDER]
        self.slabs += [(wout_hbm.at[0, :, cols(j)], wout_ref.at[:, cols(j)])
                       for j in range(wout_ref.shape[1] // WEIGHT_SLAB)]
        self.stage, self.sems = stage, sems
        self.n_ready = 0
        for i in range(min(2, len(self.slabs))):
            self._copy(i).start()

    def _copy(self, i):
        return pltpu.make_async_copy(self.slabs[i][0], self.stage.at[i % 2], self.sems.at[i % 2])

    def _need(self, n):
        while self.n_ready < n:
            i = self.n_ready
            self._copy(i).wait()
            self.slabs[i][1][...] = self.stage[i % 2].astype(jnp.bfloat16)
            if i + 2 < len(self.slabs):
                self._copy(i + 2).start()
            self.n_ready += 1

    def need_w_in(self, j):
        self._need(W_IN_USE_ORDER.index(j) + 1)

    def need_all(self):
        self._need(len(self.slabs))


def _block_kernel(x_ref, g_ref, win_hbm, lbl_ref, cw_ref, hg_ref, cg_ref, wout_hbm, fg_ref,
                  out_ref, st_ref, cu_ref, o_s, win_ref, wout_ref, stage, sems):
    n_seq, frames, d_model = x_ref.shape

    @pl.when(pl.program_id(1) == 0)
    def _reset_carries():
        st_ref[...] = jnp.zeros_like(st_ref)
        cu_ref[:, 0:SUBLANES, :] = jnp.zeros((n_seq, SUBLANES, D_CONV), jnp.float32)

    def run_step(loader):
        lbl = lbl_ref[...]
        lmax = jnp.max(lbl, axis=0, keepdims=True)
        lexp = jnp.exp(lbl - lmax)
        lb = lexp[0:1, :] / jnp.sum(lexp, axis=0, keepdims=True)
        out = _slab(x_ref[...].reshape(n_seq * frames, d_model), n_seq, lb, g_ref, win_ref,
                    cw_ref, hg_ref, cg_ref, wout_ref, fg_ref, st_ref, cu_ref, o_s, loader)
        out_ref[...] = out.reshape(n_seq, frames, d_model)

    first_step = (pl.program_id(0) == 0) & (pl.program_id(1) == 0)

    @pl.when(first_step)
    def _step_with_weight_load():
        run_step(_WeightLoader(win_hbm, wout_hbm, win_ref, wout_ref, stage, sems))

    @pl.when(jnp.logical_not(first_step))
    def _step():
        run_step(None)


@jax.jit
def kernel(x, norm_gain, w_in, lb_logits, conv_w, hgrn_norm_gain, conv_norm_gain, w_out,
           final_norm_gain):
    bsz, seq, d_model = x.shape
    depth = norm_gain.shape[0]
    assert depth == 1 and lb_logits.shape == (depth + 1, D_HGRN)
    assert w_in.shape == (depth, d_model, 4 * D_HGRN + 4 * D_CONV)
    assert w_out.shape == (depth, D_HGRN + D_CONV, d_model)
    tile = min(SEQ_TILE, seq)
    n_seq = min(BATCH_TILE, bsz)
    assert seq % tile == 0 and tile % BLOCK == 0 and bsz % n_seq == 0
    in_cols = w_in.shape[2]
    assert in_cols % WEIGHT_SLAB == 0 and d_model % WEIGHT_SLAB == 0

    full = lambda shape: pl.BlockSpec(shape, lambda b, s: (0,) * len(shape))
    grid_spec = pltpu.PrefetchScalarGridSpec(
        num_scalar_prefetch=0,
        grid=(bsz // n_seq, seq // tile),
        in_specs=[
            pl.BlockSpec((n_seq, tile, d_model), lambda b, s: (b, s, 0)),
            full((1, d_model)),
            pl.BlockSpec(memory_space=pltpu.HBM),
            full((depth + 1, D_HGRN)),
            full((CONV_WIDTH, 1, D_CONV)),
            full((1, D_HGRN)),
            full((1, D_CONV)),
            pl.BlockSpec(memory_space=pltpu.HBM),
            full((1, d_model)),
        ],
        out_specs=pl.BlockSpec((n_seq, tile, d_model), lambda b, s: (b, s, 0)),
        scratch_shapes=[
            pltpu.VMEM((n_seq, N_HEADS, HEAD_DIM, HEAD_DIM), jnp.float32),
            pltpu.VMEM((n_seq, tile + SUBLANES, D_CONV), jnp.float32),
            pltpu.VMEM((n_seq * tile, D_HGRN), jnp.float32),
            pltpu.VMEM((d_model, in_cols), jnp.bfloat16),
            pltpu.VMEM((D_HGRN + D_CONV, d_model), jnp.bfloat16),
            pltpu.VMEM((2, d_model, WEIGHT_SLAB), jnp.float32),
            pltpu.SemaphoreType.DMA((2,)),
        ],
    )
    return pl.pallas_call(
        _block_kernel,
        grid_spec=grid_spec,
        out_shape=jax.ShapeDtypeStruct(x.shape, x.dtype),
        compiler_params=pltpu.CompilerParams(
            dimension_semantics=("arbitrary", "arbitrary"),
            vmem_limit_bytes=VMEM_LIMIT_BYTES),
        name="hgrn2_shortconv_block",
    )(x, norm_gain, w_in, lb_logits, jnp.transpose(conv_w, (1, 0, 2)), hgrn_norm_gain,
      conv_norm_gain, w_out, final_norm_gain.reshape(1, d_model))
```

```python
import jax
import jax.numpy as jnp
from jax import lax
from jax.experimental import pallas as pl
from jax.experimental.pallas import tpu as pltpu

D_HGRN = 512
D_CONV = 512
HEAD_DIM = 128
N_HEADS = D_HGRN // HEAD_DIM
CHUNK = 64
BLOCK = 2 * CHUNK
CONV_WIDTH = 3
CONV_GROUP_DIM = 64
EPS = 1e-6
SEQ_TILE = 1024
X_SLOTS = 3
LANES = 128
SUBLANES = 8
VMEM_LIMIT_BYTES = 60000 * 1024
WEIGHT_SLAB = 512

_TN = (((0,), (0,)), ((), ()))


def _silu(z):
    hz = 0.5 * z
    return hz + hz * jnp.tanh(hz)


def _pre_norm(x, g_ref):
    ms = jnp.mean(x * x, axis=-1, keepdims=True)
    return (x * lax.rsqrt(ms + EPS) * g_ref[...]).astype(jnp.bfloat16)


def _slab(h, load_x, r0, lb, win_ref, cw_ref, hg_ref, cg_ref, wout_ref, fg_ref,
          st_ref, cu_ref, o_s, loader=None, before_residual=None):
    f32, bf16 = jnp.float32, jnp.bfloat16
    n_rows = h.shape[0]

    def proj(j, width):
        if loader is not None:
            loader.need_w_in(j)
        return jnp.dot(h, win_ref[:, j * width:(j + 1) * width], preferred_element_type=f32)

    q = proj(0, D_HGRN)
    f = (0.5 + 0.5 * lb) + (0.5 - 0.5 * lb) * jnp.tanh(0.5 * proj(1, D_HGRN))
    lf = jnp.log(f)
    k = 1.0 - f
    v = proj(2, D_HGRN).astype(bf16)
    u = proj(4, D_CONV)

    row = lax.broadcasted_iota(jnp.int32, (BLOCK, BLOCK), 0)
    col = lax.broadcasted_iota(jnp.int32, (BLOCK, BLOCK), 1)
    causal = row >= col
    rel = (((col >= CHUNK) & (col <= row)).astype(f32)
           - ((col < CHUNK) & (col > row)).astype(f32)).astype(bf16)
    rel2 = jnp.concatenate([rel, rel], axis=1)
    n_blocks = n_rows // BLOCK
    rows = [slice(c * BLOCK, (c + 1) * BLOCK) for c in range(n_blocks)]
    heads = [slice(hd * HEAD_DIM, (hd + 1) * HEAD_DIM) for hd in range(N_HEADS)]

    lf_hi = lf.astype(bf16)
    lf_lo = (lf - lf_hi.astype(f32)).astype(bf16)
    b = jnp.concatenate(
        [jnp.dot(rel2, jnp.concatenate([lf_hi[r], lf_lo[r]], axis=0),
                 preferred_element_type=f32) for r in rows], axis=0)
    gate_c = proj(6, D_CONV)
    eb = jnp.exp(b)
    eib = jnp.exp(-b)
    q_dec = (q * eb).astype(bf16)
    k_inv = k * eib
    head = [eib[r][0:1, :] * f[r][0:1, :] for r in rows]
    tail = [eb[r][BLOCK - 1:BLOCK, :] for r in rows]
    k_end = [(k_inv[r] * tail[c]).astype(bf16) for c, r in enumerate(rows)]

    scores = [[jnp.dot(q_dec[r, sl], k_inv[r, sl].T.astype(bf16), preferred_element_type=f32)
               for sl in heads] for r in rows]
    upd_t = [[lax.dot_general(v[r, sl], k_end[c][:, sl], _TN, preferred_element_type=f32)
              for sl in heads] for c, r in enumerate(rows)]
    gate_b = proj(5, D_CONV)
    state_before = [[None] * N_HEADS for _ in rows]
    for hd, sl in enumerate(heads):
        state_t = st_ref[hd]
        for c in range(n_blocks):
            state_before[c][hd] = (state_t * head[c][:, sl]).T.astype(bf16)
            state_t = state_t * (head[c][:, sl] * tail[c][:, sl]) + upd_t[c][hd]
        st_ref[hd] = state_t
    for c, r in enumerate(rows):
        for hd, sl in enumerate(heads):
            sc = jnp.where(causal, scores[c][hd], 0.0).astype(bf16)
            o_s[r0 + c * BLOCK:r0 + (c + 1) * BLOCK, sl] = (
                jnp.dot(sc, v[r, sl], preferred_element_type=f32)
                + jnp.dot(q_dec[r, sl], state_before[c][hd], preferred_element_type=f32))
    z_b = proj(7, D_CONV)
    z_a = proj(3, D_HGRN)

    hg = hg_ref[...]
    o_parts = []
    for sl in heads:
        oh = o_s[r0:r0 + n_rows, sl]
        oms = jnp.mean(oh * oh, axis=-1, keepdims=True)
        o_parts.append(oh * lax.rsqrt(oms + EPS) * hg[:, sl])
    o_a = (jnp.concatenate(o_parts, axis=-1) * _silu(z_a)).astype(bf16)

    cw = [cw_ref[j] for j in range(CONV_WIDTH)]
    cu = gate_c * u
    c0 = r0 + SUBLANES
    cu_ref[c0:c0 + n_rows, :] = cu
    conv = (cw[2] * cu
            + cw[1] * cu_ref[c0 - 1:c0 - 1 + n_rows, :]
            + cw[0] * cu_ref[c0 - 2:c0 - 2 + n_rows, :])
    y = gate_b * conv
    cg = cg_ref[...]
    lane = lax.broadcasted_iota(jnp.int32, (1, LANES), 1)
    first_group = lane < CONV_GROUP_DIM
    y_parts = []
    for j in range(D_CONV // LANES):
        sl = slice(j * LANES, (j + 1) * LANES)
        yj = y[:, sl]
        y2 = yj * yj
        tot = jnp.sum(y2, axis=-1, keepdims=True)
        lo = jnp.sum(jnp.where(first_group, y2, 0.0), axis=-1, keepdims=True)
        yms = jnp.where(first_group, lo, tot - lo) * (1.0 / CONV_GROUP_DIM)
        y_parts.append(yj * lax.rsqrt(yms + EPS) * cg[:, sl])
    o_b = (jnp.concatenate(y_parts, axis=-1) * _silu(z_b)).astype(bf16)

    if loader is not None:
        loader.need_all()
    mix = (jnp.dot(o_a, wout_ref[0:D_HGRN, :], preferred_element_type=f32)
           + jnp.dot(o_b, wout_ref[D_HGRN:D_HGRN + D_CONV, :], preferred_element_type=f32))
    if before_residual is not None:
        before_residual()
    res = load_x() + mix
    rms = jnp.mean(res * res, axis=-1, keepdims=True)
    return res * lax.rsqrt(rms + EPS) * fg_ref[...]


W_IN_USE_ORDER = (0, 1, 2, 4, 6, 5, 7, 3)


class _WeightLoader:
    def __init__(self, win_hbm, wout_hbm, win_ref, wout_ref, stage, sems):
        assert win_ref.shape[1] == len(W_IN_USE_ORDER) * WEIGHT_SLAB
        cols = lambda j: pl.ds(j * WEIGHT_SLAB, WEIGHT_SLAB)
        self.slabs = [(win_hbm.at[0, :, cols(j)], win_ref.at[:, cols(j)]) for j in W_IN_USE_ORDER]
        self.slabs += [(wout_hbm.at[0, :, cols(j)], wout_ref.at[:, cols(j)])
                       for j in range(wout_ref.shape[1] // WEIGHT_SLAB)]
        self.stage, self.sems = stage, sems
        self.n_ready = 0
        for i in range(min(2, len(self.slabs))):
            self._copy(i).start()

    def _copy(self, i):
        return pltpu.make_async_copy(self.slabs[i][0], self.stage.at[i % 2], self.sems.at[i % 2])

    def _need(self, n):
        while self.n_ready < n:
            i = self.n_ready
            self._copy(i).wait()
            self.slabs[i][1][...] = self.stage[i % 2].astype(jnp.bfloat16)
            if i + 2 < len(self.slabs):
                self._copy(i + 2).start()
            self.n_ready += 1

    def need_w_in(self, j):
        self._need(W_IN_USE_ORDER.index(j) + 1)

    def need_all(self):
        self._need(len(self.slabs))


def _block_kernel(x_hbm, g_ref, win_hbm, lbl_ref, cw_ref, hg_ref, cg_ref, wout_hbm, fg_ref,
                  out_ref, st_ref, cu_ref, o_s, win_ref, wout_ref, stage, sems,
                  xbuf, xsems, h_ref):
    tile = xbuf.shape[1]
    n_seq_tiles = pl.num_programs(1)
    n_tiles = pl.num_programs(0) * n_seq_tiles
    step = pl.program_id(0) * n_seq_tiles + pl.program_id(1)
    n_slots = xbuf.shape[0]

    def x_copy(t, slot):
        t = jnp.minimum(t, n_tiles - 1)
        b = t // n_seq_tiles
        s = t % n_seq_tiles
        return pltpu.make_async_copy(
            x_hbm.at[b, pl.ds(pl.multiple_of(s * tile, tile), tile), :], xbuf.at[slot],
            xsems.at[slot])

    cur = step % n_slots
    nxt = (step + 1) % n_slots
    after = (step + 2) % n_slots

    @pl.when(pl.program_id(1) == 0)
    def _reset_carries():
        st_ref[...] = jnp.zeros_like(st_ref)
        cu_ref[0:SUBLANES, :] = jnp.zeros((SUBLANES, D_CONV), jnp.float32)

    def run_step(loader):
        lbl = lbl_ref[...]
        lmax = jnp.max(lbl, axis=0, keepdims=True)
        lexp = jnp.exp(lbl - lmax)
        lb = lexp[0:1, :] / jnp.sum(lexp, axis=0, keepdims=True)

        def prepare_next_tile():
            x_copy(step + 1, nxt).wait()
            h_ref[...] = _pre_norm(xbuf[nxt], g_ref)
            x_copy(step + 2, after).start()

        out_ref[...] = _slab(h_ref[...], lambda: xbuf[cur], 0, lb, win_ref, cw_ref, hg_ref,
                             cg_ref, wout_ref, fg_ref, st_ref, cu_ref, o_s, loader,
                             prepare_next_tile)
        cu_ref[0:SUBLANES, :] = cu_ref[tile:tile + SUBLANES, :]

    @pl.when(step == 0)
    def _step_with_weight_load():
        x_copy(0, 0).start()
        x_copy(1, 1).start()
        loader = _WeightLoader(win_hbm, wout_hbm, win_ref, wout_ref, stage, sems)
        x_copy(0, 0).wait()
        h_ref[...] = _pre_norm(xbuf[0], g_ref)
        run_step(loader)

    @pl.when(step > 0)
    def _step():
        run_step(None)

    @pl.when(step == n_tiles - 1)
    def _drain():
        x_copy(step + 2, after).wait()


@jax.jit
def kernel(x, norm_gain, w_in, lb_logits, conv_w, hgrn_norm_gain, conv_norm_gain, w_out,
           final_norm_gain):
    bsz, seq, d_model = x.shape
    depth = norm_gain.shape[0]
    assert depth == 1 and lb_logits.shape == (depth + 1, D_HGRN)
    assert w_in.shape == (depth, d_model, 4 * D_HGRN + 4 * D_CONV)
    assert w_out.shape == (depth, D_HGRN + D_CONV, d_model)
    tile = min(SEQ_TILE, seq)
    assert seq % tile == 0 and tile % BLOCK == 0 and bsz * (seq // tile) >= 2
    in_cols = w_in.shape[2]
    assert in_cols % WEIGHT_SLAB == 0 and d_model % WEIGHT_SLAB == 0

    full = lambda shape: pl.BlockSpec(shape, lambda b, s: (0,) * len(shape))
    grid_spec = pltpu.PrefetchScalarGridSpec(
        num_scalar_prefetch=0,
        grid=(bsz, seq // tile),
        in_specs=[
            pl.BlockSpec(memory_space=pltpu.HBM),
            full((1, d_model)),
            pl.BlockSpec(memory_space=pltpu.HBM),
            full((depth + 1, D_HGRN)),
            full((CONV_WIDTH, 1, D_CONV)),
            full((1, D_HGRN)),
            full((1, D_CONV)),
            pl.BlockSpec(memory_space=pltpu.HBM),
            full((1, d_model)),
        ],
        out_specs=pl.BlockSpec((None, tile, d_model), lambda b, s: (b, s, 0)),
        scratch_shapes=[
            pltpu.VMEM((N_HEADS, HEAD_DIM, HEAD_DIM), jnp.float32),
            pltpu.VMEM((tile + SUBLANES, D_CONV), jnp.float32),
            pltpu.VMEM((tile, D_HGRN), jnp.float32),
            pltpu.VMEM((d_model, in_cols), jnp.bfloat16),
            pltpu.VMEM((D_HGRN + D_CONV, d_model), jnp.bfloat16),
            pltpu.VMEM((2, d_model, WEIGHT_SLAB), jnp.float32),
            pltpu.SemaphoreType.DMA((2,)),
            pltpu.VMEM((X_SLOTS, tile, d_model), jnp.float32),
            pltpu.SemaphoreType.DMA((X_SLOTS,)),
            pltpu.VMEM((tile, d_model), jnp.bfloat16),
        ],
    )
    return pl.pallas_call(
        _block_kernel,
        grid_spec=grid_spec,
        out_shape=jax.ShapeDtypeStruct(x.shape, x.dtype),
        compiler_params=pltpu.CompilerParams(
            dimension_semantics=("arbitrary", "arbitrary"),
            vmem_limit_bytes=VMEM_LIMIT_BYTES),
        name="hgrn2_shortconv_block",
    )(x, norm_gain, w_in, lb_logits, jnp.transpose(conv_w, (1, 0, 2)), hgrn_norm_gain,
      conv_norm_gain, w_out, final_norm_gain.reshape(1, d_model))
```

```python
import functools

import jax
import jax.numpy as jnp
from jax import lax
from jax.experimental import pallas as pl
from jax.experimental.pallas import tpu as pltpu

D_HGRN = 512
D_CONV = 512
HEAD_DIM = 128
N_HEADS = D_HGRN // HEAD_DIM
CHUNK = 64
BLOCK = 2 * CHUNK
CONV_WIDTH = 3
CONV_GROUP_DIM = 64
EPS = 1e-6
SEQ_TILE = 1024
SUB_TILE = 1024
LANES = 128
SUBLANES = 8
VMEM_LIMIT_BYTES = 60000 * 1024
WEIGHT_SLAB = 512

_TN = (((0,), (0,)), ((), ()))


def _silu(z):
    hz = 0.5 * z
    return hz + hz * jnp.tanh(hz)


def _slab(x, r0, lb, g_ref, win_ref, cw_ref, hg_ref, cg_ref, wout_ref, fg_ref,
          st_ref, cu_ref, o_s, loader=None):
    f32, bf16 = jnp.float32, jnp.bfloat16
    n_rows = x.shape[0]

    ms = jnp.mean(x * x, axis=-1, keepdims=True)
    h = (x * lax.rsqrt(ms + EPS) * g_ref[...]).astype(bf16)

    def proj(j, width):
        if loader is not None:
            loader.need_w_in(j)
        return jnp.dot(h, win_ref[:, j * width:(j + 1) * width], preferred_element_type=f32)

    q = proj(0, D_HGRN)
    f = (0.5 + 0.5 * lb) + (0.5 - 0.5 * lb) * jnp.tanh(0.5 * proj(1, D_HGRN))
    lf = jnp.log(f)
    k = 1.0 - f
    v = proj(2, D_HGRN).astype(bf16)
    u = proj(4, D_CONV)

    row = lax.broadcasted_iota(jnp.int32, (BLOCK, BLOCK), 0)
    col = lax.broadcasted_iota(jnp.int32, (BLOCK, BLOCK), 1)
    causal = row >= col
    rel = (((col >= CHUNK) & (col <= row)).astype(f32)
           - ((col < CHUNK) & (col > row)).astype(f32)).astype(bf16)
    rel2 = jnp.concatenate([rel, rel], axis=1)
    n_blocks = n_rows // BLOCK
    rows = [slice(c * BLOCK, (c + 1) * BLOCK) for c in range(n_blocks)]
    heads = [slice(hd * HEAD_DIM, (hd + 1) * HEAD_DIM) for hd in range(N_HEADS)]

    lf_hi = lf.astype(bf16)
    lf_lo = (lf - lf_hi.astype(f32)).astype(bf16)
    b = jnp.concatenate(
        [jnp.dot(rel2, jnp.concatenate([lf_hi[r], lf_lo[r]], axis=0),
                 preferred_element_type=f32) for r in rows], axis=0)
    gate_c = proj(6, D_CONV)
    eb = jnp.exp(b)
    eib = jnp.exp(-b)
    q_dec = (q * eb).astype(bf16)
    k_inv = k * eib
    head = [eib[r][0:1, :] * f[r][0:1, :] for r in rows]
    tail = [eb[r][BLOCK - 1:BLOCK, :] for r in rows]
    k_end = [(k_inv[r] * tail[c]).astype(bf16) for c, r in enumerate(rows)]

    scores = [[jnp.dot(q_dec[r, sl], k_inv[r, sl].T.astype(bf16), preferred_element_type=f32)
               for sl in heads] for r in rows]
    upd_t = [[lax.dot_general(v[r, sl], k_end[c][:, sl], _TN, preferred_element_type=f32)
              for sl in heads] for c, r in enumerate(rows)]
    gate_b = proj(5, D_CONV)
    state_before = [[None] * N_HEADS for _ in rows]
    for hd, sl in enumerate(heads):
        state_t = st_ref[hd]
        for c in range(n_blocks):
            state_before[c][hd] = (state_t * head[c][:, sl]).T.astype(bf16)
            state_t = state_t * (head[c][:, sl] * tail[c][:, sl]) + upd_t[c][hd]
        st_ref[hd] = state_t
    for c, r in enumerate(rows):
        for hd, sl in enumerate(heads):
            sc = jnp.where(causal, scores[c][hd], 0.0).astype(bf16)
            o_s[r0 + c * BLOCK:r0 + (c + 1) * BLOCK, sl] = (
                jnp.dot(sc, v[r, sl], preferred_element_type=f32)
                + jnp.dot(q_dec[r, sl], state_before[c][hd], preferred_element_type=f32))
    z_b = proj(7, D_CONV)
    z_a = proj(3, D_HGRN)

    hg = hg_ref[...]
    o_parts = []
    for sl in heads:
        oh = o_s[r0:r0 + n_rows, sl]
        oms = jnp.mean(oh * oh, axis=-1, keepdims=True)
        o_parts.append(oh * lax.rsqrt(oms + EPS) * hg[:, sl])
    o_a = (jnp.concatenate(o_parts, axis=-1) * _silu(z_a)).astype(bf16)

    cw = [cw_ref[j] for j in range(CONV_WIDTH)]
    cu = gate_c * u
    c0 = r0 + SUBLANES
    cu_ref[c0:c0 + n_rows, :] = cu
    conv = (cw[2] * cu
            + cw[1] * cu_ref[c0 - 1:c0 - 1 + n_rows, :]
            + cw[0] * cu_ref[c0 - 2:c0 - 2 + n_rows, :])
    y = gate_b * conv
    cg = cg_ref[...]
    lane = lax.broadcasted_iota(jnp.int32, (1, LANES), 1)
    first_group = lane < CONV_GROUP_DIM
    y_parts = []
    for j in range(D_CONV // LANES):
        sl = slice(j * LANES, (j + 1) * LANES)
        yj = y[:, sl]
        y2 = yj * yj
        tot = jnp.sum(y2, axis=-1, keepdims=True)
        lo = jnp.sum(jnp.where(first_group, y2, 0.0), axis=-1, keepdims=True)
        yms = jnp.where(first_group, lo, tot - lo) * (1.0 / CONV_GROUP_DIM)
        y_parts.append(yj * lax.rsqrt(yms + EPS) * cg[:, sl])
    o_b = (jnp.concatenate(y_parts, axis=-1) * _silu(z_b)).astype(bf16)

    if loader is not None:
        loader.need_all()
    mix = (jnp.dot(o_a, wout_ref[0:D_HGRN, :], preferred_element_type=f32)
           + jnp.dot(o_b, wout_ref[D_HGRN:D_HGRN + D_CONV, :], preferred_element_type=f32))
    res = x + mix
    rms = jnp.mean(res * res, axis=-1, keepdims=True)
    return res * lax.rsqrt(rms + EPS) * fg_ref[...]


W_IN_USE_ORDER = (0, 1, 2, 4, 6, 5, 7, 3)


class _WeightLoader:
    def __init__(self, win_hbm, wout_hbm, win_ref, wout_ref, stage, sems):
        assert win_ref.shape[1] == len(W_IN_USE_ORDER) * WEIGHT_SLAB
        cols = lambda j: pl.ds(j * WEIGHT_SLAB, WEIGHT_SLAB)
        self.slabs = [(win_hbm.at[0, :, cols(j)], win_ref.at[:, cols(j)]) for j in W_IN_USE_ORDER]
        self.slabs += [(wout_hbm.at[0, :, cols(j)], wout_ref.at[:, cols(j)])
                       for j in range(wout_ref.shape[1] // WEIGHT_SLAB)]
        self.stage, self.sems = stage, sems
        self.n_ready = 0
        for i in range(min(2, len(self.slabs))):
            self._copy(i).start()

    def _copy(self, i):
        return pltpu.make_async_copy(self.slabs[i][0], self.stage.at[i % 2], self.sems.at[i % 2])

    def _need(self, n):
        while self.n_ready < n:
            i = self.n_ready
            self._copy(i).wait()
            self.slabs[i][1][...] = self.stage[i % 2].astype(jnp.bfloat16)
            if i + 2 < len(self.slabs):
                self._copy(i + 2).start()
            self.n_ready += 1

    def need_w_in(self, j):
        self._need(W_IN_USE_ORDER.index(j) + 1)

    def need_all(self):
        self._need(len(self.slabs))


def _block_kernel(x_hbm, g_ref, win_hbm, lbl_ref, cw_ref, hg_ref, cg_ref, wout_hbm, fg_ref,
                  out_hbm, st_ref, cu_ref, o_s, win_ref, wout_ref, stage, sems, *, bsz, tile):
    d_model = x_hbm.shape[1]
    n_seq_tiles = x_hbm.shape[0] // (bsz * tile)
    sub = min(SUB_TILE, tile)

    def step_body(x_ref, out_ref):
        @pl.when(pl.program_id(1) == 0)
        def _reset_carries():
            st_ref[...] = jnp.zeros_like(st_ref)
            cu_ref[0:SUBLANES, :] = jnp.zeros((SUBLANES, D_CONV), jnp.float32)

        def run_step(loader):
            lbl = lbl_ref[...]
            lmax = jnp.max(lbl, axis=0, keepdims=True)
            lexp = jnp.exp(lbl - lmax)
            lb = lexp[0:1, :] / jnp.sum(lexp, axis=0, keepdims=True)
            for r0 in range(0, tile, sub):
                out_ref[r0:r0 + sub, :] = _slab(
                    x_ref[r0:r0 + sub, :], r0, lb, g_ref, win_ref, cw_ref, hg_ref, cg_ref,
                    wout_ref, fg_ref, st_ref, cu_ref, o_s, loader if r0 == 0 else None)
            cu_ref[0:SUBLANES, :] = cu_ref[tile:tile + SUBLANES, :]

        first_step = (pl.program_id(0) == 0) & (pl.program_id(1) == 0)

        @pl.when(first_step)
        def _step_with_weight_load():
            run_step(_WeightLoader(win_hbm, wout_hbm, win_ref, wout_ref, stage, sems))

        @pl.when(jnp.logical_not(first_step))
        def _step():
            run_step(None)

    tile_spec = pl.BlockSpec((tile, d_model), lambda b, s: (b * n_seq_tiles + s, 0))
    pltpu.emit_pipeline(step_body, grid=(bsz, n_seq_tiles), in_specs=[tile_spec],
                        out_specs=[tile_spec])(x_hbm, out_hbm)


@jax.jit
def kernel(x, norm_gain, w_in, lb_logits, conv_w, hgrn_norm_gain, conv_norm_gain, w_out,
           final_norm_gain):
    bsz, seq, d_model = x.shape
    depth = norm_gain.shape[0]
    assert depth == 1 and lb_logits.shape == (depth + 1, D_HGRN)
    assert w_in.shape == (depth, d_model, 4 * D_HGRN + 4 * D_CONV)
    assert w_out.shape == (depth, D_HGRN + D_CONV, d_model)
    tile = min(SEQ_TILE, seq)
    sub = min(SUB_TILE, tile)
    assert seq % tile == 0 and tile % sub == 0 and sub % BLOCK == 0
    in_cols = w_in.shape[2]
    assert in_cols % WEIGHT_SLAB == 0 and d_model % WEIGHT_SLAB == 0

    vmem = pl.BlockSpec(memory_space=pltpu.VMEM)
    hbm = pl.BlockSpec(memory_space=pltpu.HBM)
    out = pl.pallas_call(
        functools.partial(_block_kernel, bsz=bsz, tile=tile),
        in_specs=[hbm, vmem, hbm, vmem, vmem, vmem, vmem, hbm, vmem],
        out_specs=hbm,
        scratch_shapes=[
            pltpu.VMEM((N_HEADS, HEAD_DIM, HEAD_DIM), jnp.float32),
            pltpu.VMEM((tile + SUBLANES, D_CONV), jnp.float32),
            pltpu.VMEM((tile, D_HGRN), jnp.float32),
            pltpu.VMEM((d_model, in_cols), jnp.bfloat16),
            pltpu.VMEM((D_HGRN + D_CONV, d_model), jnp.bfloat16),
            pltpu.VMEM((2, d_model, WEIGHT_SLAB), jnp.float32),
            pltpu.SemaphoreType.DMA((2,)),
        ],
        out_shape=jax.ShapeDtypeStruct((bsz * seq, d_model), x.dtype),
        compiler_params=pltpu.CompilerParams(vmem_limit_bytes=VMEM_LIMIT_BYTES),
        name="hgrn2_shortconv_block",
    )(x.reshape(bsz * seq, d_model), norm_gain, w_in, lb_logits, jnp.transpose(conv_w, (1, 0, 2)),
      hgrn_norm_gain, conv_norm_gain, w_out, final_norm_gain.reshape(1, d_model))
    return out.reshape(bsz, seq, d_model)
```
